```python
import jax, jax.numpy as jnp
from jax import lax
import numpy as np

D_MODEL = 2048
BATCH = 16
SEQ = 2048
DEPTH = 1
DEC_BATCH = 8
DEC_SEQ = 64
PAST_LEN = 2048

CHUNK = 64
N_HEADS_M = 4
DQK = D_MODEL // 16
DV = D_MODEL // 8
D_M = N_HEADS_M * DV
SGU_CHUNK = 128
N_GROUPS_S = 4
D_S = D_MODEL // 2
GS_W = D_S // N_GROUPS_S
D_MIX = D_M + D_S
D_FF = 4 * D_MODEL
QK_W = N_HEADS_M * DQK
D_IN = 2 * QK_W + 2 * D_M + 2 * N_HEADS_M + 2 * D_S
SPLIT_IDX = (QK_W, 2 * QK_W, 2 * QK_W + D_M, 2 * QK_W + 2 * D_M,
             2 * QK_W + 2 * D_M + 2 * N_HEADS_M,
             2 * QK_W + 2 * D_M + 2 * N_HEADS_M + D_S)
EPS = 1e-6

kernel_name = "mlstm_sgu_hybrid_stream_step"


def rmsnorm(x, g):
    xf = x.astype(jnp.float32)
    y = xf * lax.rsqrt(jnp.mean(xf * xf, axis=-1, keepdims=True) + EPS)
    return (y * g.astype(jnp.float32)).astype(x.dtype)


def mlstm_chunkwise(q, k, v, ig, lf, C0, n0, m0, blk_len):
    B, S, H, _ = q.shape
    nc = S // blk_len

    def blk(a):
        a = a.reshape((B, nc, blk_len, H) + a.shape[3:])
        return jnp.moveaxis(a, (1, 3), (0, 2))

    tri = jnp.tril(jnp.ones((blk_len, blk_len), dtype=bool))

    def step(carry, xs):
        C, n, m = carry
        qb, kb, vb, ib, fb = xs
        b = jnp.cumsum(fb, axis=-1)
        a = b + m[..., None]
        d = b[..., :, None] - b[..., None, :] + ib[..., None, :]
        d = jnp.where(tri, d, -jnp.inf)
        m_t = jnp.maximum(a, jnp.max(d, axis=-1))
        w_inter = jnp.exp(a - m_t)
        s = jnp.einsum('bhtd,bhsd->bhts', qb, kb) * jnp.exp(d - m_t[..., None])
        num = (w_inter[..., None] * jnp.einsum('bhtd,bhde->bhte', qb, C)
               + jnp.einsum('bhts,bhse->bhte', s, vb))
        den = w_inter * jnp.einsum('bhtd,bhd->bht', qb, n) + jnp.sum(s, axis=-1)
        h = num / jnp.maximum(jnp.abs(den), jnp.exp(-m_t))[..., None]
        m_new = m_t[..., -1]
        ws = jnp.exp(b[..., -1:] - b + ib - m_new[..., None])
        decay = jnp.exp(b[..., -1] + m - m_new)
        C_new = decay[..., None, None] * C + jnp.einsum('bhs,bhsd,bhse->bhde', ws, kb, vb)
        n_new = decay[..., None] * n + jnp.einsum('bhs,bhsd->bhd', ws, kb)
        return (C_new, n_new, m_new), h

    (C, n, m), hs = lax.scan(step, (C0, n0, m0),
                             (blk(q), blk(k), blk(v), blk(ig), blk(lf)))
    hs = jnp.moveaxis(hs, (0, 2), (1, 3)).reshape(B, S, H, DV)
    return hs, C, n, m


def token_mix(h, C0, n0, m0, w_in, b_gate, g_mh, g_sgu, w_sp, b_sp, w_out, blk_len):
    B, S, _ = h.shape
    f32 = jnp.float32
    p = h @ w_in
    q, k, v, o, gates, u, z = jnp.split(p, SPLIT_IDX, axis=-1)
    q = q.astype(f32).reshape(B, S, N_HEADS_M, DQK)
    k = k.astype(f32).reshape(B, S, N_HEADS_M, DQK) * (DQK ** -0.5)
    v = v.astype(f32).reshape(B, S, N_HEADS_M, DV)
    gates = gates.astype(f32) + b_gate.astype(f32)
    ig = gates[..., :N_HEADS_M]
    lf = jax.nn.log_sigmoid(gates[..., N_HEADS_M:])
    hm, C, n, m = mlstm_chunkwise(q, k, v, ig, lf, C0.astype(f32), n0.astype(f32),
                                  m0.astype(f32), blk_len)
    hm = hm * lax.rsqrt(jnp.mean(hm * hm, axis=-1, keepdims=True) + EPS)
    hm = hm * g_mh.astype(f32).reshape(N_HEADS_M, DV)
    y_m = (hm.reshape(B, S, D_M) * jax.nn.sigmoid(o.astype(f32))).astype(h.dtype)
    u = jax.nn.gelu(u)
    zs = rmsnorm(jax.nn.gelu(z), g_sgu)
    lc = min(S, SGU_CHUNK)
    nch = S // lc
    mask = jnp.tril(jnp.ones((lc, lc), dtype=bool))
    w_s = jnp.where(mask, w_sp[:, :lc, :lc], 0.0)
    zr = zs.reshape(B, nch, lc, N_GROUPS_S, GS_W)
    mix = (jnp.einsum('gts,bnsge->bntge', w_s, zr)
           + b_sp[:, :lc].T[None, None, :, :, None])
    y_s = (u * mix.reshape(B, S, D_S)).astype(h.dtype)
    out = jnp.concatenate([y_m, y_s], axis=-1) @ w_out
    return out, C, n, m, zs


def layer(x, C0, n0, m0, w_in, b_gate, g_mh, g_sgu, w_sp, b_sp, w_out,
          g_norm1, g_norm2, w_ff1, w_ff2, blk_len):
    mix, C, n, m, zs = token_mix(rmsnorm(x, g_norm1), C0, n0, m0, w_in, b_gate, g_mh,
                                 g_sgu, w_sp, b_sp, w_out, blk_len)
    x = x + mix
    hid = jnp.square(jax.nn.relu(rmsnorm(x, g_norm2) @ w_ff1))
    x = x + hid @ w_ff2
    return x, C, n, m, zs


def setup_inputs(seed: int = 0) -> dict:
    key = jax.random.key(seed)
    ks = jax.random.split(key, 20)
    nrm = jax.random.normal
    f_bias = jnp.concatenate([jnp.zeros((N_HEADS_M,), jnp.float32),
                              jnp.linspace(3.0, 6.0, N_HEADS_M, dtype=jnp.float32)])
    return {
        "x_prompt": nrm(ks[0], (BATCH, SEQ, D_MODEL), jnp.float32),
        "x_sample": nrm(ks[1], (DEC_BATCH, DEC_SEQ, D_MODEL), jnp.float32),
        "state_mlstm_C": 0.1 * nrm(ks[2], (DEPTH, DEC_BATCH, N_HEADS_M, DQK, DV), jnp.float32),
        "state_mlstm_n": 0.1 * nrm(ks[3], (DEPTH, DEC_BATCH, N_HEADS_M, DQK), jnp.float32),
        "state_mlstm_m": 0.5 * nrm(ks[4], (DEPTH, DEC_BATCH, N_HEADS_M), jnp.float32),
        "w_in": nrm(ks[5], (DEPTH, D_MODEL, D_IN), jnp.float32) * D_MODEL ** -0.5,
        "b_gate": f_bias + 0.1 * nrm(ks[6], (DEPTH, 2 * N_HEADS_M), jnp.float32),
        "g_mh": 1.0 + 0.05 * nrm(ks[7], (DEPTH, D_M), jnp.float32),
        "g_sgu": 1.0 + 0.05 * nrm(ks[8], (DEPTH, D_S), jnp.float32),
        "w_sp": nrm(ks[9], (DEPTH, N_GROUPS_S, SGU_CHUNK, SGU_CHUNK), jnp.float32) * SGU_CHUNK ** -0.5,
        "b_sp": 1.0 + 0.1 * nrm(ks[10], (DEPTH, N_GROUPS_S, SGU_CHUNK), jnp.float32),
        "w_out": nrm(ks[11], (DEPTH, D_MIX, D_MODEL), jnp.float32) * D_MIX ** -0.5,
        "g_norm1": 1.0 + 0.05 * nrm(ks[12], (DEPTH, D_MODEL), jnp.float32),
        "g_norm2": 1.0 + 0.05 * nrm(ks[13], (DEPTH, D_MODEL), jnp.float32),
        "w_ff1": nrm(ks[14], (DEPTH, D_MODEL, D_FF), jnp.float32) * D_MODEL ** -0.5,
        "w_ff2": nrm(ks[15], (DEPTH, D_FF, D_MODEL), jnp.float32) * D_FF ** -0.5,
        "g_final": 1.0 + 0.05 * nrm(ks[16], (D_MODEL,), jnp.float32),
    }


def reference(x_prompt, x_sample, state_mlstm_C, state_mlstm_n, state_mlstm_m,
              w_in, b_gate, g_mh, g_sgu, w_sp, b_sp, w_out, g_norm1, g_norm2,
              w_ff1, w_ff2, g_final):
    bp = x_prompt.shape[0]
    blk_sample = x_sample.shape[1]
    yp, ys = x_prompt, x_sample
    cp_l, np_l, mp_l, cs_l, ns_l, ms_l, vs_l = [], [], [], [], [], [], []
    for l in range(DEPTH):
        params = (w_in[l], b_gate[l], g_mh[l], g_sgu[l], w_sp[l], b_sp[l], w_out[l],
                  g_norm1[l], g_norm2[l], w_ff1[l], w_ff2[l])
        C0 = jnp.zeros((bp, N_HEADS_M, DQK, DV), jnp.float32)
        n0 = jnp.zeros((bp, N_HEADS_M, DQK), jnp.float32)
        m0 = jnp.zeros((bp, N_HEADS_M), jnp.float32)
        yp, c_p, n_p, m_p, _ = layer(yp, C0, n0, m0, *params, blk_len=CHUNK)
        ys, c_s, n_s, m_s, v_s = layer(ys, state_mlstm_C[l], state_mlstm_n[l],
                                       state_mlstm_m[l], *params, blk_len=blk_sample)
        cp_l.append(c_p); np_l.append(n_p); mp_l.append(m_p)
        cs_l.append(c_s); ns_l.append(n_s); ms_l.append(m_s); vs_l.append(v_s)
    y_prompt = rmsnorm(yp, g_final)
    y_sample = rmsnorm(ys, g_final)
    C_prompt = jnp.stack(cp_l)
    n_prompt = jnp.stack(np_l)
    m_prompt = jnp.stack(mp_l)
    C_sample = jnp.stack(cs_l)
    n_sample = jnp.stack(ns_l)
    m_sample = jnp.stack(ms_l)
    sgu_v_sample = jnp.stack(vs_l)
    return (y_prompt, y_sample, C_prompt, n_prompt, m_prompt, C_sample, n_sample, m_sample, sgu_v_sample)
```

```python
import functools

import jax
import jax.numpy as jnp
from jax import lax
from jax.experimental import pallas as pl
from jax.experimental.pallas import tpu as pltpu

F32 = jnp.float32
BF16 = jnp.bfloat16

D_MODEL = 2048
N_HEADS = 4
DQK = 128
DV = 256
QK_W = N_HEADS * DQK
D_M = N_HEADS * DV
N_GROUPS = 4
GS_W = 256
D_S = N_GROUPS * GS_W
D_FF = 4 * D_MODEL
SGU_CHUNK = 128
N_GATES = 2 * N_HEADS
EPS = 1e-6

LANES = 128
BLK = 128
GATE_PAD = LANES
OFF_Q, OFF_K, OFF_V = 0, QK_W, 2 * QK_W
OFF_O = OFF_V + D_M
OFF_U = OFF_O + D_M
OFF_Z = OFF_U + D_S
OFF_G = OFF_Z + D_S
D_IN_PAD = OFF_G + GATE_PAD
AUG = DV + LANES

VMEM_LIMIT = 56 * 1024 * 1024


def _resident(shape):
    nd = len(shape)
    return pl.BlockSpec(shape, lambda *_: (0,) * nd, pipeline_mode=pl.Buffered(1))


def _inproj_kernel(x_ref, g1_ref, w_ref, gsgu_ref, qkv_ref, o_ref, u_ref, zs_ref, gates_ref):
    x = x_ref[...]
    ms = jnp.mean(x * x, axis=-1, keepdims=True)
    h = (x * lax.rsqrt(ms + EPS) * g1_ref[...]).astype(BF16)

    def proj(off, width):
        return jnp.dot(h, w_ref[:, off:off + width], preferred_element_type=F32)

    qkv_ref[:, OFF_Q:OFF_Q + QK_W] = proj(OFF_Q, QK_W).astype(BF16)
    qkv_ref[:, OFF_K:OFF_K + QK_W] = (proj(OFF_K, QK_W) * (DQK ** -0.5)).astype(BF16)
    qkv_ref[:, OFF_V:OFF_V + D_M] = proj(OFF_V, D_M).astype(BF16)
    o_ref[...] = proj(OFF_O, D_M)
    u_ref[...] = jax.nn.gelu(proj(OFF_U, D_S))
    zg = jax.nn.gelu(proj(OFF_Z, D_S))
    zms = jnp.mean(zg * zg, axis=-1, keepdims=True)
    zs_ref[...] = zg * lax.rsqrt(zms + EPS) * gsgu_ref[...]
    gates_ref[...] = proj(OFF_G, GATE_PAD)


def _inproj(x2d, g1, w_r, g_sgu, tm):
    t = x2d.shape[0]
    assert t % tm == 0
    row = lambda width: pl.BlockSpec((tm, width), lambda i: (i, 0))
    return pl.pallas_call(
        _inproj_kernel,
        grid=(t // tm,),
        in_specs=[row(D_MODEL), _resident((1, D_MODEL)), _resident((D_MODEL, D_IN_PAD)),
                  _resident((1, D_S))],
        out_specs=[row(OFF_O), row(D_M), row(D_S), row(D_S), row(GATE_PAD)],
        out_shape=[jax.ShapeDtypeStruct((t, OFF_O), BF16),
                   jax.ShapeDtypeStruct((t, D_M), F32),
                   jax.ShapeDtypeStruct((t, D_S), F32),
                   jax.ShapeDtypeStruct((t, D_S), F32),
                   jax.ShapeDtypeStruct((t, GATE_PAD), F32)],
        compiler_params=pltpu.CompilerParams(dimension_semantics=("arbitrary",),
                                             vmem_limit_bytes=VMEM_LIMIT),
        name="inproj",
    )(x2d, g1, w_r, g_sgu)


def _block_scan(x, op, fill):
    lane = lax.broadcasted_iota(jnp.int32, x.shape, 1) & (BLK - 1)
    shift = 1
    while shift < BLK:
        shifted = pltpu.roll(x, shift, axis=1)
        x = op(x, jnp.where(lane >= shift, shifted, fill))
        shift *= 2
    return x


def _col_bcast(row):
    return jnp.broadcast_to(row, (BLK, BLK)).T


def _pad_rows(a, rows):
    if a.shape[0] == rows:
        return a
    return jnp.concatenate([a, jnp.zeros((rows - a.shape[0],) + a.shape[1:], a.dtype)], axis=0)


def _mix_kernel(*refs, ts, lc, has_state):
    if has_state:
        (qkv_ref, o_ref, u_ref, zs_ref, gates_ref, x_ref, bg_ref, gmh_ref, wsp_ref, bsp_ref,
         wout_ref, c0_ref, n0_ref, m0_ref,
         x1_ref, cout_ref, nout_ref, mout_ref, caug_scr, m_scr, y_scr) = refs
    else:
        (qkv_ref, o_ref, u_ref, zs_ref, gates_ref, x_ref, bg_ref, gmh_ref, wsp_ref, bsp_ref,
         wout_ref,
         x1_ref, cout_ref, nout_ref, mout_ref, caug_scr, m_scr, y_scr) = refs
    j = pl.program_id(1)
    tp = max(ts, BLK)
    nblk = tp // BLK
    valid = min(ts, BLK)

    @pl.when(j == 0)
    def _init():
        if has_state:
            caug_scr[:, :, 0:DV] = c0_ref[0]
            for h in range(N_HEADS):
                caug_scr[h, :, DV:AUG] = _col_bcast(n0_ref[0, h:h + 1, :])
            m_scr[...] = m0_ref[0]
        else:
            caug_scr[...] = jnp.zeros(caug_scr.shape, F32)
            m_scr[...] = jnp.zeros(m_scr.shape, F32)

    gates_t = _pad_rows(gates_ref[...], tp).T
    g8 = gates_t[0:N_GATES, :] + bg_ref[:, 0:1]
    lf8 = jax.nn.log_sigmoid(g8)
    b8 = pltpu.roll(_block_scan(lf8, jnp.add, 0.0), N_HEADS, axis=0)
    r8 = g8 - b8
    cm8 = _block_scan(r8, jnp.maximum, -jnp.inf)

    row_i = lax.broadcasted_iota(jnp.int32, (BLK, BLK), 0)
    col_i = lax.broadcasted_iota(jnp.int32, (BLK, BLK), 1)
    causal = col_i <= row_i
    lane8 = lax.broadcasted_iota(jnp.int32, (N_GATES, BLK), 1)
    ones_aug = jnp.ones((BLK, LANES), BF16)

    m_prev = m_scr[:, 0:1]
    for c in range(nblk):
        lo = c * BLK
        rows_v = slice(lo, lo + valid)
        r_c = r8[:, lo:lo + BLK]
        b_c = b8[:, lo:lo + BLK]
        g_c = jnp.maximum(cm8[:, lo:lo + BLK], m_prev)
        mt_c = g_c + b_c
        g_last = g_c[:, valid - 1:valid]
        m_new = g_last + b_c[:, valid - 1:valid]
        ws_c = jnp.where(lane8 < valid, jnp.exp(r_c - g_last), 0.0)
        decay = jnp.exp(m_prev - g_last)
        for h in range(N_HEADS):
            q = _pad_rows(qkv_ref[rows_v, OFF_Q + h * DQK:OFF_Q + (h + 1) * DQK], BLK)
            k = _pad_rows(qkv_ref[rows_v, OFF_K + h * DQK:OFF_K + (h + 1) * DQK], BLK)
            v = _pad_rows(qkv_ref[rows_v, OFF_V + h * DV:OFF_V + (h + 1) * DV], BLK)
            v_aug = jnp.concatenate([v, ones_aug], axis=1)
            s = lax.dot_general(q, k, (((1,), (1,)), ((), ())), preferred_element_type=F32)
            g_col = _col_bcast(g_c[h:h + 1, :])
            arg = jnp.broadcast_to(r_c[h:h + 1, :], (BLK, BLK)) - g_col
            p = jnp.exp(jnp.where(causal, arg, -jnp.inf))
            sw = (s * p).astype(BF16)
            intra = jnp.dot(sw, v_aug, preferred_element_type=F32)
            inter = jnp.dot(q, caug_scr[h].astype(BF16), preferred_element_type=F32)
            w_col = jnp.exp(m_prev[h:h + 1, :] - g_col)
            tot = jnp.concatenate([w_col, w_col, w_col], axis=1) * inter + intra
            num = tot[:, 0:DV]
            den = tot[:, DV:AUG]
            dn = jnp.maximum(jnp.abs(den), jnp.exp(-_col_bcast(mt_c[h:h + 1, :])))
            hh = num / jnp.concatenate([dn, dn], axis=1)
            hms = jnp.mean(hh * hh, axis=-1, keepdims=True)
            hn = hh * lax.rsqrt(hms + EPS) * gmh_ref[:, h * DV:(h + 1) * DV]
            og = jax.nn.sigmoid(o_ref[rows_v, h * DV:(h + 1) * DV])
            y_scr[rows_v, h * DV:(h + 1) * DV] = (hn[0:valid] * og).astype(BF16)
            kts = (k.astype(F32).T * jnp.broadcast_to(ws_c[h:h + 1, :], (BLK, BLK))).astype(BF16)
            upd = jnp.dot(kts, v_aug, preferred_element_type=F32)
            caug_scr[h] = decay[h:h + 1, :] * caug_scr[h] + upd
        m_prev = m_new
    m_scr[...] = jnp.broadcast_to(m_prev, m_scr.shape)

    tril_lc = causal[0:lc, 0:lc]
    for gi in range(N_GROUPS):
        w_s = jnp.where(tril_lc, wsp_ref[gi][0:lc, 0:lc], 0.0).astype(BF16)
        b_col = _col_bcast(bsp_ref[gi:gi + 1, :])[0:lc]
        b_col = jnp.concatenate([b_col, b_col], axis=1)
        cols = slice(gi * GS_W, (gi + 1) * GS_W)
        for c in range(ts // lc):
            rows = slice(c * lc, (c + 1) * lc)
            mixed = jnp.dot(w_s, zs_ref[rows, cols].astype(BF16), preferred_element_type=F32) + b_col
            y_scr[rows, D_M + gi * GS_W:D_M + (gi + 1) * GS_W] = (u_ref[rows, cols] * mixed).astype(BF16)

    x1_ref[...] = x_ref[...] + jnp.dot(y_scr[...], wout_ref[...], preferred_element_type=F32)

    @pl.when(j == pl.num_programs(1) - 1)
    def _final():
        cout_ref[0] = caug_scr[:, :, 0:DV]
        for h in range(N_HEADS):
            nout_ref[0, h:h + 1, :] = caug_scr[h, :, DV:AUG].T[0:1, :]
        mout_ref[0] = m_scr[...]


def _mix(qkv, o, u, zs, gates, x2d, b_gate, g_mh, w_sp, b_sp, w_out_b, state, batch, seq, ts):
    lc = min(seq, SGU_CHUNK)
    assert seq % ts == 0 and ts % lc == 0 and (ts % BLK == 0 or ts < BLK)
    nj = seq // ts
    has_state = state is not None
    row = lambda width: pl.BlockSpec((ts, width), lambda b, j: (b * nj + j, 0))
    per_b = lambda shape: pl.BlockSpec((1,) + shape, lambda b, j: (b,) + (0,) * len(shape))
    in_specs = [row(OFF_O), row(D_M), row(D_S), row(D_S), row(GATE_PAD), row(D_MODEL),
                _resident((N_GATES, LANES)), _resident((1, D_M)),
                _resident((N_GROUPS, SGU_CHUNK, SGU_CHUNK)), _resident((N_GROUPS, SGU_CHUNK)),
                _resident((D_MODEL, D_MODEL))]
    args = [qkv, o, u, zs, gates, x2d, b_gate, g_mh, w_sp, b_sp, w_out_b]
    if has_state:
        in_specs += [per_b((N_HEADS, DQK, DV)), per_b((N_HEADS, DQK)), per_b((N_GATES, LANES))]
        args += list(state)
    return pl.pallas_call(
        functools.partial(_mix_kernel, ts=ts, lc=lc, has_state=has_state),
        grid=(batch, nj),
        in_specs=in_specs,
        out_specs=[row(D_MODEL), per_b((N_HEADS, DQK, DV)), per_b((N_HEADS, DQK)),
                   per_b((N_GATES, LANES))],
        out_shape=[jax.ShapeDtypeStruct((batch * seq, D_MODEL), F32),
                   jax.ShapeDtypeStruct((batch, N_HEADS, DQK, DV), F32),
                   jax.ShapeDtypeStruct((batch, N_HEADS, DQK), F32),
                   jax.ShapeDtypeStruct((batch, N_GATES, LANES), F32)],
        scratch_shapes=[pltpu.VMEM((N_HEADS, DQK, AUG), F32),
                        pltpu.VMEM((N_GATES, LANES), F32),
                        pltpu.VMEM((ts, D_MODEL), BF16)],
        compiler_params=pltpu.CompilerParams(dimension_semantics=("arbitrary", "arbitrary"),
                                             vmem_limit_bytes=VMEM_LIMIT),
        name="mix",
    )(*args)


def _ffn_kernel(x_ref, g2_ref, w1_ref, w2_ref, gf_ref, y_ref, xn_scr):
    kf = pl.program_id(1)

    @pl.when(kf == 0)
    def _init():
        x = x_ref[...]
        ms = jnp.mean(x * x, axis=-1, keepdims=True)
        xn_scr[...] = (x * lax.rsqrt(ms + EPS) * g2_ref[...]).astype(BF16)
        y_ref[...] = x

    hid = jnp.dot(xn_scr[...], w1_ref[...], preferred_element_type=F32)
    hid = jnp.square(jnp.maximum(hid, 0.0)).astype(BF16)
    y_ref[...] += jnp.dot(hid, w2_ref[...], preferred_element_type=F32)

    @pl.when(kf == pl.num_programs(1) - 1)
    def _final():
        y = y_ref[...]
        ms = jnp.mean(y * y, axis=-1, keepdims=True)
        y_ref[...] = y * lax.rsqrt(ms + EPS) * gf_ref[...]


def _ffn(x1, g2, w1_b, w2_b, g_final, tm, kf):
    t = x1.shape[0]
    assert t % tm == 0 and D_FF % kf == 0
    return pl.pallas_call(
        _ffn_kernel,
        grid=(t // tm, D_FF // kf),
        in_specs=[pl.BlockSpec((tm, D_MODEL), lambda i, k: (i, 0)),
                  _resident((1, D_MODEL)),
                  pl.BlockSpec((D_MODEL, kf), lambda i, k: (0, k)),
                  pl.BlockSpec((kf, D_MODEL), lambda i, k: (k, 0)),
                  _resident((1, D_MODEL))],
        out_specs=pl.BlockSpec((tm, D_MODEL), lambda i, k: (i, 0)),
        out_shape=jax.ShapeDtypeStruct((t, D_MODEL), F32),
        scratch_shapes=[pltpu.VMEM((tm, D_MODEL), BF16)],
        compiler_params=pltpu.CompilerParams(dimension_semantics=("arbitrary", "arbitrary"),
                                             vmem_limit_bytes=VMEM_LIMIT),
        name="ffn",
    )(x1, g2, w1_b, w2_b, g_final)


def _stream(x, state, params, tm_in, ts, tm_ff, kf):
    (w_r, b_gate, g_mh, g_sgu, w_sp, b_sp, w_out_b, g1, g2, w1_b, w2_b, g_final) = params
    batch, seq, _ = x.shape
    x2d = x.reshape(batch * seq, D_MODEL)
    qkv, o, u, zs, gates = _inproj(x2d, g1, w_r, g_sgu, tm_in)
    x1, c_new, n_new, m_new = _mix(qkv, o, u, zs, gates, x2d, b_gate, g_mh, w_sp, b_sp, w_out_b,
                                   state, batch, seq, ts)
    y = _ffn(x1, g2, w1_b, w2_b, g_final, tm_ff, kf)
    return (y.reshape(batch, seq, D_MODEL), c_new, n_new, m_new[:, 0:N_HEADS, 0],
            zs.reshape(batch, seq, D_S))


def kernel(x_prompt, x_sample, state_mlstm_C, state_mlstm_n, state_mlstm_m, w_in, b_gate, g_mh,
           g_sgu, w_sp, b_sp, w_out, g_norm1, g_norm2, w_ff1, w_ff2, g_final):
    depth = w_in.shape[0]
    assert depth == 1, "single-layer step"
    l = 0
    n_proj = 2 * QK_W + 2 * D_M
    w = w_in[l]
    w_r = jnp.concatenate(
        [w[:, 0:n_proj], w[:, n_proj + N_GATES:], w[:, n_proj:n_proj + N_GATES],
         jnp.zeros((D_MODEL, GATE_PAD - N_GATES), w.dtype)], axis=1).astype(BF16)
    params = (w_r,
              jnp.broadcast_to(b_gate[l][:, None], (N_GATES, LANES)),
              g_mh[l][None, :], g_sgu[l][None, :], w_sp[l], b_sp[l],
              w_out[l].astype(BF16), g_norm1[l][None, :], g_norm2[l][None, :],
              w_ff1[l].astype(BF16), w_ff2[l].astype(BF16), g_final[None, :])

    dec_batch = x_sample.shape[0]
    m0 = jnp.broadcast_to(
        jnp.pad(state_mlstm_m[l], ((0, 0), (0, N_GATES - N_HEADS)))[:, :, None],
        (dec_batch, N_GATES, LANES))
    state_s = (state_mlstm_C[l], state_mlstm_n[l], m0)

    yp, c_p, n_p, m_p, _ = _stream(x_prompt, None, params, tm_in=512, ts=256, tm_ff=1024, kf=512)
    sample_tokens = x_sample.shape[0] * x_sample.shape[1]
    ys, c_s, n_s, m_s, v_s = _stream(x_sample, state_s, params, tm_in=sample_tokens,
                                     ts=x_sample.shape[1], tm_ff=sample_tokens, kf=512)
    return (yp, ys, c_p[None], n_p[None], m_p[None], c_s[None], n_s[None], m_s[None], v_s[None])
```

```python
import functools

import jax
import jax.numpy as jnp
from jax import lax
from jax.experimental import pallas as pl
from jax.experimental.pallas import tpu as pltpu

F32 = jnp.float32
BF16 = jnp.bfloat16

D_MODEL = 2048
N_HEADS = 4
DQK = 128
DV = 256
QK_W = N_HEADS * DQK
D_M = N_HEADS * DV
N_GROUPS = 4
GS_W = 256
D_S = N_GROUPS * GS_W
D_FF = 4 * D_MODEL
SGU_CHUNK = 128
N_GATES = 2 * N_HEADS
EPS = 1e-6

LANES = 128
BLK = 128
GATE_PAD = LANES
OFF_Q, OFF_K, OFF_V = 0, QK_W, 2 * QK_W
OFF_O = OFF_V + D_M
OFF_U = OFF_O + D_M
OFF_Z = OFF_U + D_S
OFF_G = OFF_Z + D_S
D_IN_PAD = OFF_G + GATE_PAD
AUG = DV + LANES
ACT_Q, ACT_K, ACT_V = 0, QK_W, 2 * QK_W
ACT_O = ACT_V + N_HEADS * AUG
ACT_U = ACT_O + D_M
ACT_Z = ACT_U + D_S
D_ACT = ACT_Z + D_S
OP_CHUNK = 512
MIX_ORDER = ("op:0", "op:1", "op:2", "op:3", "free", "sgu:0", "state", "sgu:1", "intra", "sgu:2",
             "sgu:3")
SCAN_B, SCAN_R, SCAN_CM = 0, N_GATES, 2 * N_GATES
SCAN_ROWS = 3 * N_GATES

VMEM_LIMIT = 56 * 1024 * 1024


def _resident(shape):
    nd = len(shape)
    return pl.BlockSpec(shape, lambda *_: (0,) * nd, pipeline_mode=pl.Buffered(1))


def _rms_scale(x):
    return lax.rsqrt(jnp.mean(x * x, axis=-1, keepdims=True) + EPS)


def _block_scan(x, op, fill, blk):
    lane = lax.broadcasted_iota(jnp.int32, x.shape, 1) & (blk - 1)
    shift = 1
    while shift < blk:
        shifted = pltpu.roll(x, shift, axis=1)
        x = op(x, jnp.where(lane >= shift, shifted, fill))
        shift *= 2
    return x


def _inproj_kernel(*refs, scan_blk, emit_zs, cast_weights):
    x_ref, g1_ref, w_ref, gsgu_ref, bg_ref = refs[0:5]
    refs = refs[5:]
    if cast_weights:
        wsrc = refs[0:3]
        refs = refs[3:]
    act_ref, scan_ref = refs[0:2]
    refs = refs[2:]
    if emit_zs:
        zs_ref = refs[0]
        refs = refs[1:]
    if cast_weights:
        for src, dst in zip(wsrc, refs):
            dst[...] = src[...].astype(BF16)

    x = x_ref[...]
    h = (x * _rms_scale(x) * g1_ref[...]).astype(BF16)

    def proj(off, width):
        return jnp.dot(h, w_ref[:, off:off + width], preferred_element_type=F32)

    g8 = proj(OFF_G, GATE_PAD).T[0:N_GATES, :] + bg_ref[:, 0:1]
    b8 = pltpu.roll(_block_scan(jax.nn.log_sigmoid(g8), jnp.add, 0.0, scan_blk), N_HEADS, axis=0)
    r8 = g8 - b8
    scan_ref[SCAN_B:SCAN_B + N_GATES, :] = b8
    scan_ref[SCAN_R:SCAN_R + N_GATES, :] = r8
    scan_ref[SCAN_CM:SCAN_CM + N_GATES, :] = _block_scan(r8, jnp.maximum, -jnp.inf, scan_blk)

    zg = jax.nn.gelu(proj(OFF_Z, D_S))
    zs = zg * _rms_scale(zg) * gsgu_ref[...]
    act_ref[:, ACT_Z:ACT_Z + D_S] = zs.astype(BF16)
    if emit_zs:
        zs_ref[...] = zs
    act_ref[:, ACT_U:ACT_U + D_S] = jax.nn.gelu(proj(OFF_U, D_S)).astype(BF16)
    act_ref[:, ACT_O:ACT_O + D_M] = jax.nn.sigmoid(proj(OFF_O, D_M)).astype(BF16)
    act_ref[:, ACT_K:ACT_K + QK_W] = (proj(OFF_K, QK_W) * (DQK ** -0.5)).astype(BF16)
    act_ref[:, ACT_Q:ACT_Q + QK_W] = proj(OFF_Q, QK_W).astype(BF16)
    v = proj(OFF_V, D_M).astype(BF16)
    ones = jnp.ones((v.shape[0], LANES), BF16)
    for hd in range(N_HEADS):
        act_ref[:, ACT_V + hd * AUG:ACT_V + hd * AUG + DV] = v[:, hd * DV:(hd + 1) * DV]
        act_ref[:, ACT_V + hd * AUG + DV:ACT_V + (hd + 1) * AUG] = ones


def _inproj(x2d, g1, w_r, g_sgu, b_gate, tm, scan_blk, emit_zs, cast_srcs=None):
    t = x2d.shape[0]
    assert t % tm == 0
    n = t // tm
    row = lambda width: pl.BlockSpec((tm, width), lambda i: (i, 0))
    in_specs = [row(D_MODEL), _resident((1, D_MODEL)), _resident((D_MODEL, D_IN_PAD)),
                _resident((1, D_S)), _resident((N_GATES, LANES))]
    args = [x2d, g1, w_r, g_sgu, b_gate]
    out_specs = [row(D_ACT), pl.BlockSpec((SCAN_ROWS, tm), lambda i: (0, i))]
    out_shape = [jax.ShapeDtypeStruct((t, D_ACT), BF16), jax.ShapeDtypeStruct((SCAN_ROWS, t), F32)]
    if emit_zs:
        out_specs.append(row(D_S))
        out_shape.append(jax.ShapeDtypeStruct((t, D_S), F32))
    if cast_srcs is not None:
        for w in cast_srcs:
            assert w.shape[0] % n == 0
            slab = pl.BlockSpec((w.shape[0] // n, w.shape[1]), lambda i: (i, 0))
            in_specs.append(slab)
            args.append(w)
            out_specs.append(slab)
            out_shape.append(jax.ShapeDtypeStruct(w.shape, BF16))
    return pl.pallas_call(
        functools.partial(_inproj_kernel, scan_blk=scan_blk, emit_zs=emit_zs,
                          cast_weights=cast_srcs is not None),
        grid=(n,),
        in_specs=in_specs,
        out_specs=out_specs,
        out_shape=out_shape,
        compiler_params=pltpu.CompilerParams(dimension_semantics=("arbitrary",),
                                             vmem_limit_bytes=VMEM_LIMIT),
        name="inproj",
    )(*args)


def _col_bcast(row):
    return jnp.broadcast_to(row, (BLK, BLK)).T


def _pad_rows(a, rows):
    if a.shape[0] == rows:
        return a
    return jnp.concatenate([a, jnp.zeros((rows - a.shape[0],) + a.shape[1:], a.dtype)], axis=0)


def _pad_lanes(a, lanes):
    if a.shape[1] == lanes:
        return a
    return jnp.concatenate([a, jnp.zeros((a.shape[0], lanes - a.shape[1]), a.dtype)], axis=1)


def _mix_kernel(*refs, ts, lc, nj, n_tiles, has_state):
    (act_ref, scan_ref, x_ref, gmh_ref, wsp_ref, bsp_ref, wout_ref, g2_ref) = refs[0:8]
    refs = refs[8:]
    if has_state:
        c0_ref, n0_ref, m0_ref = refs[0:3]
        refs = refs[3:]
    (x1_ref, xn_ref, cout_ref, nout_ref, mout_ref,
     caug_scr, m_scr, ya_scr, yb_scr, wtri_scr, bcol_scr) = refs
    g = pl.program_id(0)
    j = jnp.minimum(g, n_tiles - 1) % nj
    tp = max(ts, BLK)
    nblk = tp // BLK
    valid = min(ts, BLK)

    @pl.when(g == 0)
    def _prepare():
        yb_scr[...] = jnp.zeros(yb_scr.shape, BF16)
        row_i = lax.broadcasted_iota(jnp.int32, (SGU_CHUNK, SGU_CHUNK), 0)
        col_i = lax.broadcasted_iota(jnp.int32, (SGU_CHUNK, SGU_CHUNK), 1)
        for gi in range(N_GROUPS):
            wtri_scr[gi] = jnp.where(col_i <= row_i, wsp_ref[gi], 0.0).astype(BF16)
            b_col = _col_bcast(bsp_ref[gi:gi + 1, :])
            bcol_scr[gi] = jnp.concatenate([b_col, b_col], axis=1)

    @pl.when(j == 0)
    def _init():
        if has_state:
            caug_scr[:, :, 0:DV] = c0_ref[0]
            for h in range(N_HEADS):
                caug_scr[h, :, DV:AUG] = _col_bcast(n0_ref[0, h:h + 1, :])
            m_scr[...] = m0_ref[0]
        else:
            caug_scr[...] = jnp.zeros(caug_scr.shape, F32)
            m_scr[...] = jnp.zeros(m_scr.shape, F32)

    def body(y_cur, y_prev):
        sumsq = []

        def out_proj_chunk(n):
            cols = slice(n * OP_CHUNK, (n + 1) * OP_CHUNK)
            part = x_ref[:, cols] + jnp.dot(y_prev[...], wout_ref[:, cols],
                                            preferred_element_type=F32)
            x1_ref[:, cols] = part
            sumsq.append(jnp.sum(part * part, axis=-1, keepdims=True))

        def sgu_group(gi):
            w_s = wtri_scr[gi, 0:lc, 0:lc]
            b_col = bcol_scr[gi, 0:lc, :]
            for c in range(ts // lc):
                rows = slice(c * lc, (c + 1) * lc)
                z = act_ref[rows, ACT_Z + gi * GS_W:ACT_Z + (gi + 1) * GS_W]
                u = act_ref[rows, ACT_U + gi * GS_W:ACT_U + (gi + 1) * GS_W].astype(F32)
                mixed = jnp.dot(w_s, z, preferred_element_type=F32) + b_col
                y_cur[rows, D_M + gi * GS_W:D_M + (gi + 1) * GS_W] = (u * mixed).astype(BF16)

        scan = scan_ref[...] if scan_ref.ndim == 2 else scan_ref[0]
        scan = _pad_lanes(scan, tp)
        b8 = scan[SCAN_B:SCAN_B + N_GATES]
        r8 = scan[SCAN_R:SCAN_R + N_GATES]
        cm8 = scan[SCAN_CM:SCAN_CM + N_GATES]

        row_i = lax.broadcasted_iota(jnp.int32, (BLK, BLK), 0)
        col_i = lax.broadcasted_iota(jnp.int32, (BLK, BLK), 1)
        causal = col_i <= row_i
        lane8 = lax.broadcasted_iota(jnp.int32, (N_GATES, BLK), 1)

        def load(c, off, width, h):
            rows_v = slice(c * BLK, c * BLK + valid)
            return _pad_rows(act_ref[rows_v, off + h * width:off + (h + 1) * width], BLK)

        def load_v_aug(c, h):
            return load(c, ACT_V, AUG, h)

        gate = []
        m_prev = m_scr[:, 0:1]
        for c in range(nblk):
            lo = c * BLK
            r_c = r8[:, lo:lo + BLK]
            b_c = b8[:, lo:lo + BLK]
            g_c = jnp.maximum(cm8[:, lo:lo + BLK], m_prev)
            g_last = g_c[:, valid - 1:valid]
            gate.append(dict(
                r=r_c, g=g_c, mt=g_c + b_c, m0=m_prev,
                ws=jnp.where(lane8 < valid, jnp.exp(r_c - g_last), 0.0),
                decay=jnp.exp(m_prev - g_last)))
            m_prev = g_last + b_c[:, valid - 1:valid]
        m_scr[...] = jnp.broadcast_to(m_prev, m_scr.shape)

        heads_blocks = [(c, h) for c in range(nblk) for h in range(N_HEADS)]

        s_all, upd_all, inter_all, intra_all, gcol_all = {}, {}, {}, {}, {}

        def stage_free():
            for c, h in heads_blocks:
                k = load(c, ACT_K, DQK, h)
                s_all[c, h] = lax.dot_general(load(c, ACT_Q, DQK, h), k, (((1,), (1,)), ((), ())),
                                              preferred_element_type=F32)
                ws_row = jnp.broadcast_to(gate[c]["ws"][h:h + 1, :], (BLK, BLK))
                kts = (k.astype(F32).T * ws_row).astype(BF16)
                upd_all[c, h] = jnp.dot(kts, load_v_aug(c, h), preferred_element_type=F32)

        def stage_state():
            for h in range(N_HEADS):
                caug = caug_scr[h]
                for c in range(nblk):
                    inter_all[c, h] = jnp.dot(load(c, ACT_Q, DQK, h), caug.astype(BF16),
                                              preferred_element_type=F32)
                    caug = gate[c]["decay"][h:h + 1, :] * caug + upd_all[c, h]
                caug_scr[h] = caug

        def stage_intra():
            for c, h in heads_blocks:
                g_col = _col_bcast(gate[c]["g"][h:h + 1, :])
                arg = jnp.broadcast_to(gate[c]["r"][h:h + 1, :], (BLK, BLK)) - g_col
                p = jnp.exp(jnp.where(causal, arg, -jnp.inf))
                sw = (s_all[c, h] * p).astype(BF16)
                intra_all[c, h] = jnp.dot(sw, load_v_aug(c, h), preferred_element_type=F32)
                gcol_all[c, h] = g_col

        steps = {"free": stage_free, "state": stage_state, "intra": stage_intra}
        for name in MIX_ORDER:
            kind, _, arg = name.partition(":")
            if kind == "op":
                out_proj_chunk(int(arg))
            elif kind == "sgu":
                sgu_group(int(arg))
            else:
                steps[kind]()
        assert len(sumsq) * OP_CHUNK == D_MODEL

        for c, h in heads_blocks:
            rows_v = slice(c * BLK, c * BLK + valid)
            w_col = jnp.exp(gate[c]["m0"][h:h + 1, :] - gcol_all[c, h])
            tot = jnp.concatenate([w_col, w_col, w_col], axis=1) * inter_all[c, h] + intra_all[c, h]
            num = tot[:, 0:DV]
            den = tot[:, DV:AUG]
            dn = jnp.maximum(jnp.abs(den), jnp.exp(-_col_bcast(gate[c]["mt"][h:h + 1, :])))
            hh = num / jnp.concatenate([dn, dn], axis=1)
            hn = hh * _rms_scale(hh) * gmh_ref[:, h * DV:(h + 1) * DV]
            og = act_ref[rows_v, ACT_O + h * DV:ACT_O + (h + 1) * DV].astype(F32)
            y_cur[rows_v, h * DV:(h + 1) * DV] = (hn[0:valid] * og).astype(BF16)

        ms = functools.reduce(jnp.add, sumsq) * (1.0 / D_MODEL)
        xn_ref[...] = (x1_ref[...] * lax.rsqrt(ms + EPS) * g2_ref[...]).astype(BF16)

    @pl.when(g % 2 == 0)
    def _even():
        body(ya_scr, yb_scr)

    @pl.when(g % 2 == 1)
    def _odd():
        body(yb_scr, ya_scr)

    @pl.when((j == nj - 1) & (g < n_tiles))
    def _final():
        cout_ref[0] = caug_scr[:, :, 0:DV]
        for h in range(N_HEADS):
            nout_ref[0, h:h + 1, :] = caug_scr[h, :, DV:AUG].T[0:1, :]
        mout_ref[0] = m_scr[...]


def _mix(act, scan, x2d, g_mh, w_sp, b_sp, w_out_b, g2, state, batch, seq, ts):
    lc = min(seq, SGU_CHUNK)
    assert seq % ts == 0 and ts % lc == 0 and (ts % BLK == 0 or ts < BLK)
    nj = seq // ts
    n_tiles = batch * nj
    has_state = state is not None
    cur = lambda g: jnp.minimum(g, n_tiles - 1)
    prev = lambda g: jnp.maximum(g - 1, 0)
    row_cur = lambda width: pl.BlockSpec((ts, width), lambda g: (cur(g), 0))
    row_prev = lambda width: pl.BlockSpec((ts, width), lambda g: (prev(g), 0))
    per_b = lambda shape: pl.BlockSpec((1,) + shape, lambda g: (cur(g) // nj,) + (0,) * len(shape))
    if scan.ndim == 2:
        scan_spec = pl.BlockSpec((SCAN_ROWS, ts), lambda g: (0, cur(g)))
    else:
        scan_spec = pl.BlockSpec((1, SCAN_ROWS, ts), lambda g: (cur(g), 0, 0))
    in_specs = [row_cur(D_ACT), scan_spec, row_prev(D_MODEL), _resident((1, D_M)),
                _resident((N_GROUPS, SGU_CHUNK, SGU_CHUNK)), _resident((N_GROUPS, SGU_CHUNK)),
                _resident((D_MODEL, D_MODEL)), _resident((1, D_MODEL))]
    args = [act, scan, x2d, g_mh, w_sp, b_sp, w_out_b, g2]
    if has_state:
        in_specs += [per_b((N_HEADS, DQK, DV)), per_b((N_HEADS, DQK)), per_b((N_GATES, LANES))]
        args += list(state)
    return pl.pallas_call(
        functools.partial(_mix_kernel, ts=ts, lc=lc, nj=nj, n_tiles=n_tiles, has_state=has_state),
        grid=(n_tiles + 1,),
        in_specs=in_specs,
        out_specs=[row_prev(D_MODEL), row_prev(D_MODEL), per_b((N_HEADS, DQK, DV)),
                   per_b((N_HEADS, DQK)), per_b((N_GATES, LANES))],
        out_shape=[jax.ShapeDtypeStruct((batch * seq, D_MODEL), F32),
                   jax.ShapeDtypeStruct((batch * seq, D_MODEL), BF16),
                   jax.ShapeDtypeStruct((batch, N_HEADS, DQK, DV), F32),
                   jax.ShapeDtypeStruct((batch, N_HEADS, DQK), F32),
                   jax.ShapeDtypeStruct((batch, N_GATES, LANES), F32)],
        scratch_shapes=[pltpu.VMEM((N_HEADS, DQK, AUG), F32),
                        pltpu.VMEM((N_GATES, LANES), F32),
                        pltpu.VMEM((ts, D_MODEL), BF16),
                        pltpu.VMEM((ts, D_MODEL), BF16),
                        pltpu.VMEM((N_GROUPS, SGU_CHUNK, SGU_CHUNK), BF16),
                        pltpu.VMEM((N_GROUPS, SGU_CHUNK, GS_W), F32)],
        compiler_params=pltpu.CompilerParams(dimension_semantics=("arbitrary",),
                                             vmem_limit_bytes=VMEM_LIMIT),
        name="mix",
    )(*args)


def _ffn_kernel(xn_ref, x1c_ref, w1_ref, w2_ref, gf_ref, y_ref):
    kf = pl.program_id(1)
    rows = x1c_ref.shape[0]

    def delta():
        hid = jnp.dot(xn_ref[...], w1_ref[...], preferred_element_type=F32)
        hid = jnp.square(jnp.maximum(hid, 0.0)).astype(BF16)
        return jnp.dot(hid, w2_ref[...], preferred_element_type=F32)

    @pl.when(kf == 0)
    def _first():
        y_ref[...] = delta()

    @pl.when(kf != 0)
    def _rest():
        y_ref[...] += delta()

    slab = pl.ds(pl.multiple_of(kf * rows, rows), rows)
    y_ref[slab, :] += x1c_ref[...]

    @pl.when(kf == pl.num_programs(1) - 1)
    def _final():
        y = y_ref[...]
        y_ref[...] = y * _rms_scale(y) * gf_ref[...]


def _ffn(xn, x1, w1_b, w2_b, g_final, tm, kf):
    t = x1.shape[0]
    assert t % tm == 0 and D_FF % kf == 0
    nk = D_FF // kf
    assert tm % nk == 0
    return pl.pallas_call(
        _ffn_kernel,
        grid=(t // tm, nk),
        in_specs=[pl.BlockSpec((tm, D_MODEL), lambda i, k: (i, 0)),
                  pl.BlockSpec((tm // nk, D_MODEL), lambda i, k: (i * nk + k, 0)),
                  pl.BlockSpec((D_MODEL, kf), lambda i, k: (0, k)),
                  pl.BlockSpec((kf, D_MODEL), lambda i, k: (k, 0)),
                  _resident((1, D_MODEL))],
        out_specs=pl.BlockSpec((tm, D_MODEL), lambda i, k: (i, 0)),
        out_shape=jax.ShapeDtypeStruct((t, D_MODEL), F32),
        compiler_params=pltpu.CompilerParams(dimension_semantics=("arbitrary", "arbitrary"),
                                             vmem_limit_bytes=VMEM_LIMIT),
        name="ffn",
    )(xn, x1, w1_b, w2_b, g_final)


def kernel(x_prompt, x_sample, state_mlstm_C, state_mlstm_n, state_mlstm_m, w_in, b_gate, g_mh,
           g_sgu, w_sp, b_sp, w_out, g_norm1, g_norm2, w_ff1, w_ff2, g_final):
    depth = w_in.shape[0]
    assert depth == 1, "single-layer step"
    l = 0
    n_proj = 2 * QK_W + 2 * D_M
    w = w_in[l]
    w_r = jnp.concatenate(
        [w[:, 0:n_proj].astype(BF16), w[:, n_proj + N_GATES:].astype(BF16),
         w[:, n_proj:n_proj + N_GATES].astype(BF16),
         jnp.zeros((D_MODEL, GATE_PAD - N_GATES), BF16)], axis=1)
    bg = jnp.broadcast_to(b_gate[l][:, None], (N_GATES, LANES))
    g1, g2, gf = g_norm1[l][None, :], g_norm2[l][None, :], g_final[None, :]
    gmh, gsgu = g_mh[l][None, :], g_sgu[l][None, :]

    bp, sp, _ = x_prompt.shape
    xp2d = x_prompt.reshape(bp * sp, D_MODEL)
    act, scan, w1_b, w2_b, w_out_b = _inproj(xp2d, g1, w_r, gsgu, bg, tm=512, scan_blk=BLK,
                                             emit_zs=False, cast_srcs=(w_ff1[l], w_ff2[l], w_out[l]))
    x1, xn, c_p, n_p, m_p = _mix(act, scan, xp2d, gmh, w_sp[l], b_sp[l], w_out_b, g2, None,
                                 bp, sp, ts=256)
    yp = _ffn(xn, x1, w1_b, w2_b, gf, tm=1024, kf=1024)

    bs, ss, _ = x_sample.shape
    xs2d = x_sample.reshape(bs * ss, D_MODEL)
    act_s, scan_s, zs_s = _inproj(xs2d, g1, w_r, gsgu, bg, tm=bs * ss, scan_blk=ss, emit_zs=True)
    scan_s = scan_s.reshape(SCAN_ROWS, bs, ss).transpose(1, 0, 2)
    m0 = jnp.broadcast_to(
        jnp.pad(state_mlstm_m[l], ((0, 0), (0, N_GATES - N_HEADS)))[:, :, None],
        (bs, N_GATES, LANES))
    x1s, xns, c_s, n_s, m_s = _mix(act_s, scan_s, xs2d, gmh, w_sp[l], b_sp[l], w_out_b, g2,
                                   (state_mlstm_C[l], state_mlstm_n[l], m0), bs, ss, ts=ss)
    ys = _ffn(xns, x1s, w1_b, w2_b, gf, tm=bs * ss, kf=1024)

    return (yp.reshape(bp, sp, D_MODEL), ys.reshape(bs, ss, D_MODEL),
            c_p[None], n_p[None], m_p[:, 0:N_HEADS, 0][None],
            c_s[None], n_s[None], m_s[:, 0:N_HEADS, 0][None],
            zs_s.reshape(bs, ss, D_S)[None])
```

```python
import functools

import jax
import jax.numpy as jnp
from jax import lax
from jax.experimental import pallas as pl
from jax.experimental.pallas import tpu as pltpu

F32 = jnp.float32
BF16 = jnp.bfloat16

D_MODEL = 2048
N_HEADS = 4
DQK = 128
DV = 256
QK_W = N_HEADS * DQK
D_M = N_HEADS * DV
N_GROUPS = 4
GS_W = 256
D_S = N_GROUPS * GS_W
D_FF = 4 * D_MODEL
SGU_CHUNK = 128
N_GATES = 2 * N_HEADS
EPS = 1e-6

LANES = 128
BLK = 128
GATE_PAD = LANES
N_QKVO = 2 * QK_W + 2 * D_M
OFF_Q, OFF_K, OFF_V = 0, QK_W, 2 * QK_W
OFF_O = OFF_V + D_M
OFF_U, OFF_Z = 0, D_S
AUG = DV + LANES
ACT_Q, ACT_K, ACT_V = 0, QK_W, 2 * QK_W
ACT_O = ACT_V + N_HEADS * AUG
ACT_U = ACT_O + D_M
ACT_Z = ACT_U + D_S
D_ACT = ACT_Z + D_S
OP_CHUNK = 512
MIX_ORDER = ("op:0", "op:1", "op:2", "op:3", "free", "sgu:0", "state", "sgu:1", "intra", "sgu:2",
             "sgu:3")
SCAN_B, SCAN_R, SCAN_CM = 0, N_GATES, 2 * N_GATES
SCAN_ROWS = 3 * N_GATES

VMEM_LIMIT = 56 * 1024 * 1024


def _resident(shape):
    nd = len(shape)
    return pl.BlockSpec(shape, lambda *_: (0,) * nd, pipeline_mode=pl.Buffered(1))


def _rms_scale(x):
    return lax.rsqrt(jnp.mean(x * x, axis=-1, keepdims=True) + EPS)


def _block_scan(x, op, fill, blk):
    lane = lax.broadcasted_iota(jnp.int32, x.shape, 1) & (blk - 1)
    shift = 1
    while shift < blk:
        shifted = pltpu.roll(x, shift, axis=1)
        x = op(x, jnp.where(lane >= shift, shifted, fill))
        shift *= 2
    return x


def _inproj_kernel(*refs, scan_blk, emit_zs, cast_weights):
    x_ref, g1_ref, wa_ref, wuz_ref, wg_ref, gsgu_ref, bg_ref = refs[0:7]
    refs = refs[7:]
    if cast_weights:
        wsrc = refs[0:3]
        refs = refs[3:]
    act_ref, scan_ref = refs[0:2]
    refs = refs[2:]
    if emit_zs:
        zs_ref = refs[0]
        refs = refs[1:]
    if cast_weights:
        for src, dst in zip(wsrc, refs):
            dst[...] = src[...].astype(BF16)

    x = x_ref[...]
    h = (x * _rms_scale(x) * g1_ref[...]).astype(BF16)

    def proj(w_ref, off, width):
        return jnp.dot(h, w_ref[:, off:off + width], preferred_element_type=F32)

    g8 = proj(wg_ref, 0, GATE_PAD).T[0:N_GATES, :] + bg_ref[:, 0:1]
    b8 = pltpu.roll(_block_scan(jax.nn.log_sigmoid(g8), jnp.add, 0.0, scan_blk), N_HEADS, axis=0)
    r8 = g8 - b8
    scan_ref[SCAN_B:SCAN_B + N_GATES, :] = b8
    scan_ref[SCAN_R:SCAN_R + N_GATES, :] = r8
    scan_ref[SCAN_CM:SCAN_CM + N_GATES, :] = _block_scan(r8, jnp.maximum, -jnp.inf, scan_blk)

    zg = jax.nn.gelu(proj(wuz_ref, OFF_Z, D_S))
    zs = zg * _rms_scale(zg) * gsgu_ref[...]
    act_ref[:, ACT_Z:ACT_Z + D_S] = zs.astype(BF16)
    if emit_zs:
        zs_ref[...] = zs
    act_ref[:, ACT_U:ACT_U + D_S] = jax.nn.gelu(proj(wuz_ref, OFF_U, D_S)).astype(BF16)
    act_ref[:, ACT_O:ACT_O + D_M] = jax.nn.sigmoid(proj(wa_ref, OFF_O, D_M)).astype(BF16)
    act_ref[:, ACT_K:ACT_K + QK_W] = (proj(wa_ref, OFF_K, QK_W) * (DQK ** -0.5)).astype(BF16)
    act_ref[:, ACT_Q:ACT_Q + QK_W] = proj(wa_ref, OFF_Q, QK_W).astype(BF16)
    v = proj(wa_ref, OFF_V, D_M).astype(BF16)
    ones = jnp.ones((v.shape[0], LANES), BF16)
    for hd in range(N_HEADS):
        act_ref[:, ACT_V + hd * AUG:ACT_V + hd * AUG + DV] = v[:, hd * DV:(hd + 1) * DV]
        act_ref[:, ACT_V + hd * AUG + DV:ACT_V + (hd + 1) * AUG] = ones


def _inproj(x2d, g1, w_parts, g_sgu, b_gate, tm, scan_blk, emit_zs, cast_srcs=None):
    t = x2d.shape[0]
    assert t % tm == 0
    n = t // tm
    row = lambda width: pl.BlockSpec((tm, width), lambda i: (i, 0))
    in_specs = [row(D_MODEL), _resident((1, D_MODEL)), _resident((D_MODEL, N_QKVO)),
                _resident((D_MODEL, 2 * D_S)), _resident((D_MODEL, GATE_PAD)),
                _resident((1, D_S)), _resident((N_GATES, LANES))]
    args = [x2d, g1, *w_parts, g_sgu, b_gate]
    out_specs = [row(D_ACT), pl.BlockSpec((SCAN_ROWS, tm), lambda i: (0, i))]
    out_shape = [jax.ShapeDtypeStruct((t, D_ACT), BF16), jax.ShapeDtypeStruct((SCAN_ROWS, t), F32)]
    if emit_zs:
        out_specs.append(row(D_S))
        out_shape.append(jax.ShapeDtypeStruct((t, D_S), F32))
    if cast_srcs is not None:
        for w in cast_srcs:
            assert w.shape[0] % n == 0
            slab = pl.BlockSpec((w.shape[0] // n, w.shape[1]), lambda i: (i, 0))
            in_specs.append(slab)
            args.append(w)
            out_specs.append(slab)
            out_shape.append(jax.ShapeDtypeStruct(w.shape, BF16))
    return pl.pallas_call(
        functools.partial(_inproj_kernel, scan_blk=scan_blk, emit_zs=emit_zs,
                          cast_weights=cast_srcs is not None),
        grid=(n,),
        in_specs=in_specs,
        out_specs=out_specs,
        out_shape=out_shape,
        compiler_params=pltpu.CompilerParams(dimension_semantics=("arbitrary",),
                                             vmem_limit_bytes=VMEM_LIMIT),
        name="inproj",
    )(*args)


def _col_bcast(row):
    return jnp.broadcast_to(row, (BLK, BLK)).T


def _pad_rows(a, rows):
    if a.shape[0] == rows:
        return a
    return jnp.concatenate([a, jnp.zeros((rows - a.shape[0],) + a.shape[1:], a.dtype)], axis=0)


def _pad_lanes(a, lanes):
    if a.shape[1] == lanes:
        return a
    return jnp.concatenate([a, jnp.zeros((a.shape[0], lanes - a.shape[1]), a.dtype)], axis=1)


def _mix_kernel(*refs, ts, lc, nj, n_tiles, has_state):
    (act_ref, scan_ref, x_ref, gmh_ref, wsp_ref, bsp_ref, wout_ref, g2_ref) = refs[0:8]
    refs = refs[8:]
    if has_state:
        c0_ref, n0_ref, m0_ref = refs[0:3]
        refs = refs[3:]
    (x1_ref, xn_ref, cout_ref, nout_ref, mout_ref,
     caug_scr, m_scr, ya_scr, yb_scr, wtri_scr, bcol_scr) = refs
    g = pl.program_id(0)
    j = jnp.minimum(g, n_tiles - 1) % nj
    tp = max(ts, BLK)
    nblk = tp // BLK
    valid = min(ts, BLK)

    @pl.when(g == 0)
    def _prepare():
        yb_scr[...] = jnp.zeros(yb_scr.shape, BF16)
        row_i = lax.broadcasted_iota(jnp.int32, (SGU_CHUNK, SGU_CHUNK), 0)
        col_i = lax.broadcasted_iota(jnp.int32, (SGU_CHUNK, SGU_CHUNK), 1)
        for gi in range(N_GROUPS):
            wtri_scr[gi] = jnp.where(col_i <= row_i, wsp_ref[gi], 0.0).astype(BF16)
            b_col = _col_bcast(bsp_ref[gi:gi + 1, :])
            bcol_scr[gi] = jnp.concatenate([b_col, b_col], axis=1)

    @pl.when(j == 0)
    def _init():
        if has_state:
            caug_scr[:, :, 0:DV] = c0_ref[0]
            for h in range(N_HEADS):
                caug_scr[h, :, DV:AUG] = _col_bcast(n0_ref[0, h:h + 1, :])
            m_scr[...] = m0_ref[0]
        else:
            caug_scr[...] = jnp.zeros(caug_scr.shape, F32)
            m_scr[...] = jnp.zeros(m_scr.shape, F32)

    def body(y_cur, y_prev):
        sumsq = []

        def out_proj_chunk(n):
            cols = slice(n * OP_CHUNK, (n + 1) * OP_CHUNK)
            part = x_ref[:, cols] + jnp.dot(y_prev[...], wout_ref[:, cols],
                                            preferred_element_type=F32)
            x1_ref[:, cols] = part
            sumsq.append(jnp.sum(part * part, axis=-1, keepdims=True))

        def sgu_group(gi):
            w_s = wtri_scr[gi, 0:lc, 0:lc]
            b_col = bcol_scr[gi, 0:lc, :]
            for c in range(ts // lc):
                rows = slice(c * lc, (c + 1) * lc)
                z = act_ref[rows, ACT_Z + gi * GS_W:ACT_Z + (gi + 1) * GS_W]
                u = act_ref[rows, ACT_U + gi * GS_W:ACT_U + (gi + 1) * GS_W].astype(F32)
                mixed = jnp.dot(w_s, z, preferred_element_type=F32) + b_col
                y_cur[rows, D_M + gi * GS_W:D_M + (gi + 1) * GS_W] = (u * mixed).astype(BF16)

        scan = scan_ref[...] if scan_ref.ndim == 2 else scan_ref[0]
        scan = _pad_lanes(scan, tp)
        b8 = scan[SCAN_B:SCAN_B + N_GATES]
        r8 = scan[SCAN_R:SCAN_R + N_GATES]
        cm8 = scan[SCAN_CM:SCAN_CM + N_GATES]

        row_i = lax.broadcasted_iota(jnp.int32, (BLK, BLK), 0)
        col_i = lax.broadcasted_iota(jnp.int32, (BLK, BLK), 1)
        causal = col_i <= row_i
        lane8 = lax.broadcasted_iota(jnp.int32, (N_GATES, BLK), 1)

        def load(c, off, width, h):
            rows_v = slice(c * BLK, c * BLK + valid)
            return _pad_rows(act_ref[rows_v, off + h * width:off + (h + 1) * width], BLK)

        def load_v_aug(c, h):
            return load(c, ACT_V, AUG, h)

        gate = []
        m_prev = m_scr[:, 0:1]
        for c in range(nblk):
            lo = c * BLK
            r_c = r8[:, lo:lo + BLK]
            b_c = b8[:, lo:lo + BLK]
            g_c = jnp.maximum(cm8[:, lo:lo + BLK], m_prev)
            g_last = g_c[:, valid - 1:valid]
            gate.append(dict(
                r=r_c, g=g_c, mt=g_c + b_c, m0=m_prev,
                ws=jnp.where(lane8 < valid, jnp.exp(r_c - g_last), 0.0),
                decay=jnp.exp(m_prev - g_last)))
            m_prev = g_last + b_c[:, valid - 1:valid]
        m_scr[...] = jnp.broadcast_to(m_prev, m_scr.shape)

        heads_blocks = [(c, h) for c in range(nblk) for h in range(N_HEADS)]

        s_all, upd_all, inter_all, intra_all, gcol_all = {}, {}, {}, {}, {}

        def stage_free():
            for c, h in heads_blocks:
                k = load(c, ACT_K, DQK, h)
                s_all[c, h] = lax.dot_general(load(c, ACT_Q, DQK, h), k, (((1,), (1,)), ((), ())),
                                              preferred_element_type=F32)
                ws_row = jnp.broadcast_to(gate[c]["ws"][h:h + 1, :], (BLK, BLK))
                kts = (k.astype(F32).T * ws_row).astype(BF16)
                upd_all[c, h] = jnp.dot(kts, load_v_aug(c, h), preferred_element_type=F32)

        def stage_state():
            for h in range(N_HEADS):
                caug = caug_scr[h]
                for c in range(nblk):
                    inter_all[c, h] = jnp.dot(load(c, ACT_Q, DQK, h), caug.astype(BF16),
                                              preferred_element_type=F32)
                    caug = gate[c]["decay"][h:h + 1, :] * caug + upd_all[c, h]
                caug_scr[h] = caug

        def stage_intra():
            for c, h in heads_blocks:
                g_col = _col_bcast(gate[c]["g"][h:h + 1, :])
                arg = jnp.broadcast_to(gate[c]["r"][h:h + 1, :], (BLK, BLK)) - g_col
                p = jnp.exp(jnp.where(causal, arg, -jnp.inf))
                sw = (s_all[c, h] * p).astype(BF16)
                intra_all[c, h] = jnp.dot(sw, load_v_aug(c, h), preferred_element_type=F32)
                gcol_all[c, h] = g_col

        steps = {"free": stage_free, "state": stage_state, "intra": stage_intra}
        for name in MIX_ORDER:
            kind, _, arg = name.partition(":")
            if kind == "op":
                out_proj_chunk(int(arg))
            elif kind == "sgu":
                sgu_group(int(arg))
            else:
                steps[kind]()
        assert len(sumsq) * OP_CHUNK == D_MODEL

        for c, h in heads_blocks:
            rows_v = slice(c * BLK, c * BLK + valid)
            w_col = jnp.exp(gate[c]["m0"][h:h + 1, :] - gcol_all[c, h])
            tot = jnp.concatenate([w_col, w_col, w_col], axis=1) * inter_all[c, h] + intra_all[c, h]
            num = tot[:, 0:DV]
            den = tot[:, DV:AUG]
            dn = jnp.maximum(jnp.abs(den), jnp.exp(-_col_bcast(gate[c]["mt"][h:h + 1, :])))
            hh = num / jnp.concatenate([dn, dn], axis=1)
            hn = hh * _rms_scale(hh) * gmh_ref[:, h * DV:(h + 1) * DV]
            og = act_ref[rows_v, ACT_O + h * DV:ACT_O + (h + 1) * DV].astype(F32)
            y_cur[rows_v, h * DV:(h + 1) * DV] = (hn[0:valid] * og).astype(BF16)

        ms = functools.reduce(jnp.add, sumsq) * (1.0 / D_MODEL)
        xn_ref[...] = (x1_ref[...] * lax.rsqrt(ms + EPS) * g2_ref[...]).astype(BF16)

    @pl.when(g % 2 == 0)
    def _even():
        body(ya_scr, yb_scr)

    @pl.when(g % 2 == 1)
    def _odd():
        body(yb_scr, ya_scr)

    @pl.when((j == nj - 1) & (g < n_tiles))
    def _final():
        cout_ref[0] = caug_scr[:, :, 0:DV]
        for h in range(N_HEADS):
            nout_ref[0, h:h + 1, :] = caug_scr[h, :, DV:AUG].T[0:1, :]
        mout_ref[0] = m_scr[...]


def _mix(act, scan, x2d, g_mh, w_sp, b_sp, w_out_b, g2, state, batch, seq, ts):
    lc = min(seq, SGU_CHUNK)
    assert seq % ts == 0 and ts % lc == 0 and (ts % BLK == 0 or ts < BLK)
    nj = seq // ts
    n_tiles = batch * nj
    has_state = state is not None
    cur = lambda g: jnp.minimum(g, n_tiles - 1)
    prev = lambda g: jnp.maximum(g - 1, 0)
    row_cur = lambda width: pl.BlockSpec((ts, width), lambda g: (cur(g), 0))
    row_prev = lambda width: pl.BlockSpec((ts, width), lambda g: (prev(g), 0))
    per_b = lambda shape: pl.BlockSpec((1,) + shape, lambda g: (cur(g) // nj,) + (0,) * len(shape))
    if scan.ndim == 2:
        scan_spec = pl.BlockSpec((SCAN_ROWS, ts), lambda g: (0, cur(g)))
    else:
        scan_spec = pl.BlockSpec((1, SCAN_ROWS, ts), lambda g: (cur(g), 0, 0))
    in_specs = [row_cur(D_ACT), scan_spec, row_prev(D_MODEL), _resident((1, D_M)),
                _resident((N_GROUPS, SGU_CHUNK, SGU_CHUNK)), _resident((N_GROUPS, SGU_CHUNK)),
                _resident((D_MODEL, D_MODEL)), _resident((1, D_MODEL))]
    args = [act, scan, x2d, g_mh, w_sp, b_sp, w_out_b, g2]
    if has_state:
        in_specs += [per_b((N_HEADS, DQK, DV)), per_b((N_HEADS, DQK)), per_b((N_GATES, LANES))]
        args += list(state)
    return pl.pallas_call(
        functools.partial(_mix_kernel, ts=ts, lc=lc, nj=nj, n_tiles=n_tiles, has_state=has_state),
        grid=(n_tiles + 1,),
        in_specs=in_specs,
        out_specs=[row_prev(D_MODEL), row_prev(D_MODEL), per_b((N_HEADS, DQK, DV)),
                   per_b((N_HEADS, DQK)), per_b((N_GATES, LANES))],
        out_shape=[jax.ShapeDtypeStruct((batch * seq, D_MODEL), F32),
                   jax.ShapeDtypeStruct((batch * seq, D_MODEL), BF16),
                   jax.ShapeDtypeStruct((batch, N_HEADS, DQK, DV), F32),
                   jax.ShapeDtypeStruct((batch, N_HEADS, DQK), F32),
                   jax.ShapeDtypeStruct((batch, N_GATES, LANES), F32)],
        scratch_shapes=[pltpu.VMEM((N_HEADS, DQK, AUG), F32),
                        pltpu.VMEM((N_GATES, LANES), F32),
                        pltpu.VMEM((ts, D_MODEL), BF16),
                        pltpu.VMEM((ts, D_MODEL), BF16),
                        pltpu.VMEM((N_GROUPS, SGU_CHUNK, SGU_CHUNK), BF16),
                        pltpu.VMEM((N_GROUPS, SGU_CHUNK, GS_W), F32)],
        compiler_params=pltpu.CompilerParams(dimension_semantics=("arbitrary",),
                                             vmem_limit_bytes=VMEM_LIMIT),
        name="mix",
    )(*args)


def _ffn_kernel(xn_ref, x1c_ref, w1_ref, w2_ref, gf_ref, y_ref):
    kf = pl.program_id(1)
    rows = x1c_ref.shape[0]

    def delta():
        hid = jnp.dot(xn_ref[...], w1_ref[...], preferred_element_type=F32)
        hid = jnp.square(jnp.maximum(hid, 0.0)).astype(BF16)
        return jnp.dot(hid, w2_ref[...], preferred_element_type=F32)

    @pl.when(kf == 0)
    def _first():
        y_ref[...] = delta()

    @pl.when(kf != 0)
    def _rest():
        y_ref[...] += delta()

    slab = pl.ds(pl.multiple_of(kf * rows, rows), rows)
    y_ref[slab, :] += x1c_ref[...]

    @pl.when(kf == pl.num_programs(1) - 1)
    def _final():
        y = y_ref[...]
        y_ref[...] = y * _rms_scale(y) * gf_ref[...]


def _ffn(xn, x1, w1_b, w2_b, g_final, tm, kf):
    t = x1.shape[0]
    assert t % tm == 0 and D_FF % kf == 0
    nk = D_FF // kf
    assert tm % nk == 0
    return pl.pallas_call(
        _ffn_kernel,
        grid=(t // tm, nk),
        in_specs=[pl.BlockSpec((tm, D_MODEL), lambda i, k: (i, 0)),
                  pl.BlockSpec((tm // nk, D_MODEL), lambda i, k: (i * nk + k, 0)),
                  pl.BlockSpec((D_MODEL, kf), lambda i, k: (0, k)),
                  pl.BlockSpec((kf, D_MODEL), lambda i, k: (k, 0)),
                  _resident((1, D_MODEL))],
        out_specs=pl.BlockSpec((tm, D_MODEL), lambda i, k: (i, 0)),
        out_shape=jax.ShapeDtypeStruct((t, D_MODEL), F32),
        compiler_params=pltpu.CompilerParams(dimension_semantics=("arbitrary", "arbitrary"),
                                             vmem_limit_bytes=VMEM_LIMIT),
        name="ffn",
    )(xn, x1, w1_b, w2_b, g_final)


def kernel(x_prompt, x_sample, state_mlstm_C, state_mlstm_n, state_mlstm_m, w_in, b_gate, g_mh,
           g_sgu, w_sp, b_sp, w_out, g_norm1, g_norm2, w_ff1, w_ff2, g_final):
    depth = w_in.shape[0]
    assert depth == 1, "single-layer step"
    l = 0
    w = w_in[l]
    w_r = (w[:, 0:N_QKVO].astype(BF16),
           w[:, N_QKVO + N_GATES:].astype(BF16),
           jnp.pad(w[:, N_QKVO:N_QKVO + N_GATES].astype(BF16), ((0, 0), (0, GATE_PAD - N_GATES))))
    bg = jnp.broadcast_to(b_gate[l][:, None], (N_GATES, LANES))
    g1, g2, gf = g_norm1[l][None, :], g_norm2[l][None, :], g_final[None, :]
    gmh, gsgu = g_mh[l][None, :], g_sgu[l][None, :]

    bp, sp, _ = x_prompt.shape
    xp2d = x_prompt.reshape(bp * sp, D_MODEL)
    act, scan, w1_b, w2_b, w_out_b = _inproj(xp2d, g1, w_r, gsgu, bg, tm=512, scan_blk=BLK,
                                             emit_zs=False, cast_srcs=(w_ff1[l], w_ff2[l], w_out[l]))
    x1, xn, c_p, n_p, m_p = _mix(act, scan, xp2d, gmh, w_sp[l], b_sp[l], w_out_b, g2, None,
                                 bp, sp, ts=512)
    yp = _ffn(xn, x1, w1_b, w2_b, gf, tm=1024, kf=1024)

    bs, ss, _ = x_sample.shape
    xs2d = x_sample.reshape(bs * ss, D_MODEL)
    act_s, scan_s, zs_s = _inproj(xs2d, g1, w_r, gsgu, bg, tm=bs * ss, scan_blk=ss, emit_zs=True)
    scan_s = scan_s.reshape(SCAN_ROWS, bs, ss).transpose(1, 0, 2)
    m0 = jnp.broadcast_to(
        jnp.pad(state_mlstm_m[l], ((0, 0), (0, N_GATES - N_HEADS)))[:, :, None],
        (bs, N_GATES, LANES))
    x1s, xns, c_s, n_s, m_s = _mix(act_s, scan_s, xs2d, gmh, w_sp[l], b_sp[l], w_out_b, g2,
                                   (state_mlstm_C[l], state_mlstm_n[l], m0), bs, ss, ts=ss)
    ys = _ffn(xns, x1s, w1_b, w2_b, gf, tm=bs * ss, kf=1024)

    return (yp.reshape(bp, sp, D_MODEL), ys.reshape(bs, ss, D_MODEL),
            c_p[None], n_p[None], m_p[:, 0:N_HEADS, 0][None],
            c_s[None], n_s[None], m_s[:, 0:N_HEADS, 0][None],
            zs_s.reshape(bs, ss, D_S)[None])
```

```python
import functools

import jax
import jax.numpy as jnp
from jax import lax
from jax.experimental import pallas as pl
from jax.experimental.pallas import tpu as pltpu

F32 = jnp.float32
BF16 = jnp.bfloat16

D_MODEL = 2048
N_HEADS = 4
DQK = 128
DV = 256
QK_W = N_HEADS * DQK
D_M = N_HEADS * DV
N_GROUPS = 4
GS_W = 256
D_S = N_GROUPS * GS_W
D_FF = 4 * D_MODEL
SGU_CHUNK = 128
N_GATES = 2 * N_HEADS
EPS = 1e-6

LANES = 128
BLK = 128
GATE_PAD = LANES
N_QKVO = 2 * QK_W + 2 * D_M
OFF_Q, OFF_K, OFF_V = 0, QK_W, 2 * QK_W
OFF_O = OFF_V + D_M
OFF_U, OFF_Z = 0, D_S
AUG = DV + LANES
ACT_Q, ACT_K, ACT_V = 0, QK_W, 2 * QK_W
ACT_O = ACT_V + N_HEADS * AUG
ACT_U = ACT_O + D_M
ACT_Z = ACT_U + D_S
D_ACT = ACT_Z + D_S
OP_CHUNK = 512
MIX_ORDER = ("op:0", "op:1", "op:2", "op:3", "free", "sgu:0", "state", "sgu:1", "intra", "sgu:2",
             "sgu:3")
SCAN_B, SCAN_R, SCAN_CM = 0, N_GATES, 2 * N_GATES
SCAN_ROWS = 3 * N_GATES

VMEM_LIMIT = 56 * 1024 * 1024


def _resident(shape):
    nd = len(shape)
    return pl.BlockSpec(shape, lambda *_: (0,) * nd, pipeline_mode=pl.Buffered(1))


def _rms_scale(x):
    return lax.rsqrt(jnp.mean(x * x, axis=-1, keepdims=True) + EPS)


def _block_scan(x, op, fill, blk):
    lane = lax.broadcasted_iota(jnp.int32, x.shape, 1) & (blk - 1)
    shift = 1
    while shift < blk:
        shifted = pltpu.roll(x, shift, axis=1)
        x = op(x, jnp.where(lane >= shift, shifted, fill))
        shift *= 2
    return x


def _inproj_kernel(*refs, scan_blk, emit_zs, cast_weights):
    x_ref, g1_ref, wa_ref, wuz_ref, wg_ref, gsgu_ref, bg_ref = refs[0:7]
    refs = refs[7:]
    if cast_weights:
        wsrc = refs[0:3]
        refs = refs[3:]
    act_ref, scan_ref = refs[0:2]
    refs = refs[2:]
    if emit_zs:
        zs_ref = refs[0]
        refs = refs[1:]
    if cast_weights:
        for src, dst in zip(wsrc, refs):
            dst[...] = src[...].astype(BF16)

    x = x_ref[...]
    h = (x * _rms_scale(x) * g1_ref[...]).astype(BF16)

    def proj(w_ref, off, width):
        return jnp.dot(h, w_ref[:, off:off + width], preferred_element_type=F32)

    g8 = proj(wg_ref, 0, GATE_PAD).T[0:N_GATES, :] + bg_ref[:, 0:1]
    b8 = pltpu.roll(_block_scan(jax.nn.log_sigmoid(g8), jnp.add, 0.0, scan_blk), N_HEADS, axis=0)
    r8 = g8 - b8
    scan_ref[SCAN_B:SCAN_B + N_GATES, :] = b8
    scan_ref[SCAN_R:SCAN_R + N_GATES, :] = r8
    scan_ref[SCAN_CM:SCAN_CM + N_GATES, :] = _block_scan(r8, jnp.maximum, -jnp.inf, scan_blk)

    def group_z():
        zg = jax.nn.gelu(proj(wuz_ref, OFF_Z, D_S))
        zs = zg * _rms_scale(zg) * gsgu_ref[...]
        act_ref[:, ACT_Z:ACT_Z + D_S] = zs.astype(BF16)
        if emit_zs:
            zs_ref[...] = zs

    def group_u():
        act_ref[:, ACT_U:ACT_U + D_S] = jax.nn.gelu(proj(wuz_ref, OFF_U, D_S)).astype(BF16)

    def group_o():
        act_ref[:, ACT_O:ACT_O + D_M] = jax.nn.sigmoid(proj(wa_ref, OFF_O, D_M)).astype(BF16)

    def group_k():
        act_ref[:, ACT_K:ACT_K + QK_W] = (proj(wa_ref, OFF_K, QK_W) * (DQK ** -0.5)).astype(BF16)

    def group_q():
        act_ref[:, ACT_Q:ACT_Q + QK_W] = proj(wa_ref, OFF_Q, QK_W).astype(BF16)

    def group_v():
        v = proj(wa_ref, OFF_V, D_M).astype(BF16)
        ones = jnp.ones((v.shape[0], LANES), BF16)
        for hd in range(N_HEADS):
            act_ref[:, ACT_V + hd * AUG:ACT_V + hd * AUG + DV] = v[:, hd * DV:(hd + 1) * DV]
            act_ref[:, ACT_V + hd * AUG + DV:ACT_V + (hd + 1) * AUG] = ones

    for group in (group_z, group_v, group_u, group_k, group_o, group_q):
        group()


def _inproj(x2d, g1, w_parts, g_sgu, b_gate, tm, scan_blk, emit_zs, cast_srcs=None):
    t = x2d.shape[0]
    assert t % tm == 0
    n = t // tm
    row = lambda width: pl.BlockSpec((tm, width), lambda i: (i, 0))
    in_specs = [row(D_MODEL), _resident((1, D_MODEL)), _resident((D_MODEL, N_QKVO)),
                _resident((D_MODEL, 2 * D_S)), _resident((D_MODEL, GATE_PAD)),
                _resident((1, D_S)), _resident((N_GATES, LANES))]
    args = [x2d, g1, *w_parts, g_sgu, b_gate]
    out_specs = [row(D_ACT), pl.BlockSpec((SCAN_ROWS, tm), lambda i: (0, i))]
    out_shape = [jax.ShapeDtypeStruct((t, D_ACT), BF16), jax.ShapeDtypeStruct((SCAN_ROWS, t), F32)]
    if emit_zs:
        out_specs.append(row(D_S))
        out_shape.append(jax.ShapeDtypeStruct((t, D_S), F32))
    if cast_srcs is not None:
        for w in cast_srcs:
            assert w.shape[0] % n == 0
            slab = pl.BlockSpec((w.shape[0] // n, w.shape[1]), lambda i: (i, 0))
            in_specs.append(slab)
            args.append(w)
            out_specs.append(slab)
            out_shape.append(jax.ShapeDtypeStruct(w.shape, BF16))
    return pl.pallas_call(
        functools.partial(_inproj_kernel, scan_blk=scan_blk, emit_zs=emit_zs,
                          cast_weights=cast_srcs is not None),
        grid=(n,),
        in_specs=in_specs,
        out_specs=out_specs,
        out_shape=out_shape,
        compiler_params=pltpu.CompilerParams(dimension_semantics=("arbitrary",),
                                             vmem_limit_bytes=VMEM_LIMIT),
        name="inproj",
    )(*args)


def _col_bcast(row):
    return jnp.broadcast_to(row, (BLK, BLK)).T


def _pad_rows(a, rows):
    if a.shape[0] == rows:
        return a
    return jnp.concatenate([a, jnp.zeros((rows - a.shape[0],) + a.shape[1:], a.dtype)], axis=0)


def _pad_lanes(a, lanes):
    if a.shape[1] == lanes:
        return a
    return jnp.concatenate([a, jnp.zeros((a.shape[0], lanes - a.shape[1]), a.dtype)], axis=1)


def _mix_kernel(*refs, ts, lc, nj, n_tiles, has_state):
    (act_ref, scan_ref, x_ref, gmh_ref, wsp_ref, bsp_ref, wout_ref, g2_ref) = refs[0:8]
    refs = refs[8:]
    if has_state:
        c0_ref, n0_ref, m0_ref = refs[0:3]
        refs = refs[3:]
    (x1_ref, xn_ref, cout_ref, nout_ref, mout_ref,
     caug_scr, m_scr, ya_scr, yb_scr, wtri_scr, bcol_scr) = refs
    g = pl.program_id(0)
    j = jnp.minimum(g, n_tiles - 1) % nj
    tp = max(ts, BLK)
    nblk = tp // BLK
    valid = min(ts, BLK)

    @pl.when(g == 0)
    def _prepare():
        yb_scr[...] = jnp.zeros(yb_scr.shape, BF16)
        row_i = lax.broadcasted_iota(jnp.int32, (SGU_CHUNK, SGU_CHUNK), 0)
        col_i = lax.broadcasted_iota(jnp.int32, (SGU_CHUNK, SGU_CHUNK), 1)
        for gi in range(N_GROUPS):
            wtri_scr[gi] = jnp.where(col_i <= row_i, wsp_ref[gi], 0.0).astype(BF16)
            b_col = _col_bcast(bsp_ref[gi:gi + 1, :])
            bcol_scr[gi] = jnp.concatenate([b_col, b_col], axis=1)

    @pl.when(j == 0)
    def _init():
        if has_state:
            caug_scr[:, :, 0:DV] = c0_ref[0]
            for h in range(N_HEADS):
                caug_scr[h, :, DV:AUG] = _col_bcast(n0_ref[0, h:h + 1, :])
            m_scr[...] = m0_ref[0]
        else:
            caug_scr[...] = jnp.zeros(caug_scr.shape, F32)
            m_scr[...] = jnp.zeros(m_scr.shape, F32)

    def body(y_cur, y_prev):
        sumsq = []

        def out_proj_chunk(n):
            cols = slice(n * OP_CHUNK, (n + 1) * OP_CHUNK)
            part = x_ref[:, cols] + jnp.dot(y_prev[...], wout_ref[:, cols],
                                            preferred_element_type=F32)
            x1_ref[:, cols] = part
            sumsq.append(jnp.sum(part * part, axis=-1, keepdims=True))

        def sgu_group(gi):
            w_s = wtri_scr[gi, 0:lc, 0:lc]
            b_col = bcol_scr[gi, 0:lc, :]
            for c in range(ts // lc):
                rows = slice(c * lc, (c + 1) * lc)
                z = act_ref[rows, ACT_Z + gi * GS_W:ACT_Z + (gi + 1) * GS_W]
                u = act_ref[rows, ACT_U + gi * GS_W:ACT_U + (gi + 1) * GS_W].astype(F32)
                mixed = jnp.dot(w_s, z, preferred_element_type=F32) + b_col
                y_cur[rows, D_M + gi * GS_W:D_M + (gi + 1) * GS_W] = (u * mixed).astype(BF16)

        scan = scan_ref[...] if scan_ref.ndim == 2 else scan_ref[0]
        scan = _pad_lanes(scan, tp)
        b8 = scan[SCAN_B:SCAN_B + N_GATES]
        r8 = scan[SCAN_R:SCAN_R + N_GATES]
        cm8 = scan[SCAN_CM:SCAN_CM + N_GATES]

        row_i = lax.broadcasted_iota(jnp.int32, (BLK, BLK), 0)
        col_i = lax.broadcasted_iota(jnp.int32, (BLK, BLK), 1)
        causal = col_i <= row_i
        lane8 = lax.broadcasted_iota(jnp.int32, (N_GATES, BLK), 1)

        def load(c, off, width, h):
            rows_v = slice(c * BLK, c * BLK + valid)
            return _pad_rows(act_ref[rows_v, off + h * width:off + (h + 1) * width], BLK)

        def load_v_aug(c, h):
            return load(c, ACT_V, AUG, h)

        gate = []
        m_prev = m_scr[:, 0:1]
        for c in range(nblk):
            lo = c * BLK
            r_c = r8[:, lo:lo + BLK]
            b_c = b8[:, lo:lo + BLK]
            g_c = jnp.maximum(cm8[:, lo:lo + BLK], m_prev)
            g_last = g_c[:, valid - 1:valid]
            gate.append(dict(
                r=r_c, g=g_c, mt=g_c + b_c, m0=m_prev,
                ws=jnp.where(lane8 < valid, jnp.exp(r_c - g_last), 0.0),
                decay=jnp.exp(m_prev - g_last)))
            m_prev = g_last + b_c[:, valid - 1:valid]
        m_scr[...] = jnp.broadcast_to(m_prev, m_scr.shape)

        heads_blocks = [(c, h) for c in range(nblk) for h in range(N_HEADS)]

        s_all, upd_all, inter_all, intra_all, gcol_all = {}, {}, {}, {}, {}

        def stage_free():
            for c, h in heads_blocks:
                k = load(c, ACT_K, DQK, h)
                s_all[c, h] = lax.dot_general(load(c, ACT_Q, DQK, h), k, (((1,), (1,)), ((), ())),
                                              preferred_element_type=F32)
                ws_row = jnp.broadcast_to(gate[c]["ws"][h:h + 1, :], (BLK, BLK))
                kts = (k.astype(F32).T * ws_row).astype(BF16)
                upd_all[c, h] = jnp.dot(kts, load_v_aug(c, h), preferred_element_type=F32)

        def stage_state():
            for h in range(N_HEADS):
                caug = caug_scr[h]
                for c in range(nblk):
                    inter_all[c, h] = jnp.dot(load(c, ACT_Q, DQK, h), caug.astype(BF16),
                                              preferred_element_type=F32)
                    caug = gate[c]["decay"][h:h + 1, :] * caug + upd_all[c, h]
                caug_scr[h] = caug

        def stage_intra():
            for c, h in heads_blocks:
                g_col = _col_bcast(gate[c]["g"][h:h + 1, :])
                arg = jnp.broadcast_to(gate[c]["r"][h:h + 1, :], (BLK, BLK)) - g_col
                p = jnp.exp(jnp.where(causal, arg, -jnp.inf))
                sw = (s_all[c, h] * p).astype(BF16)
                intra_all[c, h] = jnp.dot(sw, load_v_aug(c, h), preferred_element_type=F32)
                gcol_all[c, h] = g_col

        steps = {"free": stage_free, "state": stage_state, "intra": stage_intra}
        for name in MIX_ORDER:
            kind, _, arg = name.partition(":")
            if kind == "op":
                out_proj_chunk(int(arg))
            elif kind == "sgu":
                sgu_group(int(arg))
            else:
                steps[kind]()
        assert len(sumsq) * OP_CHUNK == D_MODEL

        for c, h in heads_blocks:
            rows_v = slice(c * BLK, c * BLK + valid)
            w_col = jnp.exp(gate[c]["m0"][h:h + 1, :] - gcol_all[c, h])
            tot = jnp.concatenate([w_col, w_col, w_col], axis=1) * inter_all[c, h] + intra_all[c, h]
            num = tot[:, 0:DV]
            den = tot[:, DV:AUG]
            dn = jnp.maximum(jnp.abs(den), jnp.exp(-_col_bcast(gate[c]["mt"][h:h + 1, :])))
            hh = num / jnp.concatenate([dn, dn], axis=1)
            hn = hh * _rms_scale(hh) * gmh_ref[:, h * DV:(h + 1) * DV]
            og = act_ref[rows_v, ACT_O + h * DV:ACT_O + (h + 1) * DV].astype(F32)
            y_cur[rows_v, h * DV:(h + 1) * DV] = (hn[0:valid] * og).astype(BF16)

        ms = functools.reduce(jnp.add, sumsq) * (1.0 / D_MODEL)
        xn_ref[...] = (x1_ref[...] * lax.rsqrt(ms + EPS) * g2_ref[...]).astype(BF16)

    @pl.when(g % 2 == 0)
    def _even():
        body(ya_scr, yb_scr)

    @pl.when(g % 2 == 1)
    def _odd():
        body(yb_scr, ya_scr)

    @pl.when((j == nj - 1) & (g < n_tiles))
    def _final():
        cout_ref[0] = caug_scr[:, :, 0:DV]
        for h in range(N_HEADS):
            nout_ref[0, h:h + 1, :] = caug_scr[h, :, DV:AUG].T[0:1, :]
        mout_ref[0] = m_scr[...]


def _mix(act, scan, x2d, g_mh, w_sp, b_sp, w_out_b, g2, state, batch, seq, ts):
    lc = min(seq, SGU_CHUNK)
    assert seq % ts == 0 and ts % lc == 0 and (ts % BLK == 0 or ts < BLK)
    nj = seq // ts
    n_tiles = batch * nj
    has_state = state is not None
    cur = lambda g: jnp.minimum(g, n_tiles - 1)
    prev = lambda g: jnp.maximum(g - 1, 0)
    row_cur = lambda width: pl.BlockSpec((ts, width), lambda g: (cur(g), 0))
    row_prev = lambda width: pl.BlockSpec((ts, width), lambda g: (prev(g), 0))
    per_b = lambda shape: pl.BlockSpec((1,) + shape, lambda g: (cur(g) // nj,) + (0,) * len(shape))
    if scan.ndim == 2:
        scan_spec = pl.BlockSpec((SCAN_ROWS, ts), lambda g: (0, cur(g)))
    else:
        scan_spec = pl.BlockSpec((1, SCAN_ROWS, ts), lambda g: (cur(g), 0, 0))
    in_specs = [row_cur(D_ACT), scan_spec, row_prev(D_MODEL), _resident((1, D_M)),
                _resident((N_GROUPS, SGU_CHUNK, SGU_CHUNK)), _resident((N_GROUPS, SGU_CHUNK)),
                _resident((D_MODEL, D_MODEL)), _resident((1, D_MODEL))]
    args = [act, scan, x2d, g_mh, w_sp, b_sp, w_out_b, g2]
    if has_state:
        in_specs += [per_b((N_HEADS, DQK, DV)), per_b((N_HEADS, DQK)), per_b((N_GATES, LANES))]
        args += list(state)
    return pl.pallas_call(
        functools.partial(_mix_kernel, ts=ts, lc=lc, nj=nj, n_tiles=n_tiles, has_state=has_state),
        grid=(n_tiles + 1,),
        in_specs=in_specs,
        out_specs=[row_prev(D_MODEL), row_prev(D_MODEL), per_b((N_HEADS, DQK, DV)),
                   per_b((N_HEADS, DQK)), per_b((N_GATES, LANES))],
        out_shape=[jax.ShapeDtypeStruct((batch * seq, D_MODEL), F32),
                   jax.ShapeDtypeStruct((batch * seq, D_MODEL), BF16),
                   jax.ShapeDtypeStruct((batch, N_HEADS, DQK, DV), F32),
                   jax.ShapeDtypeStruct((batch, N_HEADS, DQK), F32),
                   jax.ShapeDtypeStruct((batch, N_GATES, LANES), F32)],
        scratch_shapes=[pltpu.VMEM((N_HEADS, DQK, AUG), F32),
                        pltpu.VMEM((N_GATES, LANES), F32),
                        pltpu.VMEM((ts, D_MODEL), BF16),
                        pltpu.VMEM((ts, D_MODEL), BF16),
                        pltpu.VMEM((N_GROUPS, SGU_CHUNK, SGU_CHUNK), BF16),
                        pltpu.VMEM((N_GROUPS, SGU_CHUNK, GS_W), F32)],
        compiler_params=pltpu.CompilerParams(dimension_semantics=("arbitrary",),
                                             vmem_limit_bytes=VMEM_LIMIT),
        name="mix",
    )(*args)


def _ffn_kernel(xn_ref, x1c_ref, w1_ref, w2_ref, gf_ref, y_ref):
    kf = pl.program_id(1)
    rows = x1c_ref.shape[0]

    def delta():
        hid = jnp.dot(xn_ref[...], w1_ref[...], preferred_element_type=F32)
        hid = jnp.square(jnp.maximum(hid, 0.0)).astype(BF16)
        return jnp.dot(hid, w2_ref[...], preferred_element_type=F32)

    last = pl.num_programs(1) - 1
    slab = pl.ds(pl.multiple_of(kf * rows, rows), rows)

    @pl.when(kf == 0)
    def _first():
        y_ref[...] = delta()
        y_ref[slab, :] += x1c_ref[...]

    @pl.when((kf != 0) & (kf != last))
    def _middle():
        y_ref[...] += delta()
        y_ref[slab, :] += x1c_ref[...]

    @pl.when(kf == last)
    def _last():
        y_ref[slab, :] += x1c_ref[...]
        y = y_ref[...] + delta()
        y_ref[...] = y * _rms_scale(y) * gf_ref[...]


def _ffn(xn, x1, w1_b, w2_b, g_final, tm, kf):
    t = x1.shape[0]
    assert t % tm == 0 and D_FF % kf == 0
    nk = D_FF // kf
    assert tm % nk == 0 and nk >= 2
    return pl.pallas_call(
        _ffn_kernel,
        grid=(t // tm, nk),
        in_specs=[pl.BlockSpec((tm, D_MODEL), lambda i, k: (i, 0)),
                  pl.BlockSpec((tm // nk, D_MODEL), lambda i, k: (i * nk + k, 0)),
                  pl.BlockSpec((D_MODEL, kf), lambda i, k: (0, k)),
                  pl.BlockSpec((kf, D_MODEL), lambda i, k: (k, 0)),
                  _resident((1, D_MODEL))],
        out_specs=pl.BlockSpec((tm, D_MODEL), lambda i, k: (i, 0)),
        out_shape=jax.ShapeDtypeStruct((t, D_MODEL), F32),
        compiler_params=pltpu.CompilerParams(dimension_semantics=("arbitrary", "arbitrary"),
                                             vmem_limit_bytes=VMEM_LIMIT),
        name="ffn",
    )(xn, x1, w1_b, w2_b, g_final)


def kernel(x_prompt, x_sample, state_mlstm_C, state_mlstm_n, state_mlstm_m, w_in, b_gate, g_mh,
           g_sgu, w_sp, b_sp, w_out, g_norm1, g_norm2, w_ff1, w_ff2, g_final):
    depth = w_in.shape[0]
    assert depth == 1, "single-layer step"
    l = 0
    d_in = w_in.shape[2]

    def w_cols(lo, hi):
        return lax.slice(w_in, (l, 0, lo), (l + 1, D_MODEL, hi)).reshape(D_MODEL, hi - lo).astype(BF16)

    w_r = (w_cols(0, N_QKVO), w_cols(N_QKVO + N_GATES, d_in),
           jnp.pad(w_cols(N_QKVO, N_QKVO + N_GATES), ((0, 0), (0, GATE_PAD - N_GATES))))
    bg = jnp.broadcast_to(b_gate[l][:, None], (N_GATES, LANES))
    g1, g2, gf = g_norm1[l][None, :], g_norm2[l][None, :], g_final[None, :]
    gmh, gsgu = g_mh[l][None, :], g_sgu[l][None, :]

    bp, sp, _ = x_prompt.shape
    xp2d = x_prompt.reshape(bp * sp, D_MODEL)
    act, scan, w1_b, w2_b, w_out_b = _inproj(xp2d, g1, w_r, gsgu, bg, tm=512, scan_blk=BLK,
                                             emit_zs=False, cast_srcs=(w_ff1[l], w_ff2[l], w_out[l]))
    x1, xn, c_p, n_p, m_p = _mix(act, scan, xp2d, gmh, w_sp[l], b_sp[l], w_out_b, g2, None,
                                 bp, sp, ts=512)
    yp = _ffn(xn, x1, w1_b, w2_b, gf, tm=1024, kf=1024)

    bs, ss, _ = x_sample.shape
    xs2d = x_sample.reshape(bs * ss, D_MODEL)
    act_s, scan_s, zs_s = _inproj(xs2d, g1, w_r, gsgu, bg, tm=bs * ss, scan_blk=ss, emit_zs=True)
    scan_s = scan_s.reshape(SCAN_ROWS, bs, ss).transpose(1, 0, 2)
    m0 = jnp.broadcast_to(
        jnp.pad(state_mlstm_m[l], ((0, 0), (0, N_GATES - N_HEADS)))[:, :, None],
        (bs, N_GATES, LANES))
    x1s, xns, c_s, n_s, m_s = _mix(act_s, scan_s, xs2d, gmh, w_sp[l], b_sp[l], w_out_b, g2,
                                   (state_mlstm_C[l], state_mlstm_n[l], m0), bs, ss, ts=ss)
    ys = _ffn(xns, x1s, w1_b, w2_b, gf, tm=bs * ss, kf=1024)

    return (yp.reshape(bp, sp, D_MODEL), ys.reshape(bs, ss, D_MODEL),
            c_p[None], n_p[None], m_p[:, 0:N_HEADS, 0][None],
            c_s[None], n_s[None], m_s[:, 0:N_HEADS, 0][None],
            zs_s.reshape(bs, ss, D_S)[None])
```

```python
import functools

import jax
import jax.numpy as jnp
from jax import lax
from jax.experimental import pallas as pl
from jax.experimental.pallas import tpu as pltpu

F32 = jnp.float32
BF16 = jnp.bfloat16

D_MODEL = 2048
N_HEADS = 4
DQK = 128
DV = 256
QK_W = N_HEADS * DQK
D_M = N_HEADS * DV
N_GROUPS = 4
GS_W = 256
D_S = N_GROUPS * GS_W
D_FF = 4 * D_MODEL
SGU_CHUNK = 128
N_GATES = 2 * N_HEADS
EPS = 1e-6

LANES = 128
BLK = 128
BF16_ROWS = 16
N_QKVO = 2 * QK_W + 2 * D_M
OFF_Q, OFF_K, OFF_V = 0, QK_W, 2 * QK_W
OFF_O = OFF_V + D_M
OFF_G = N_QKVO
WA_ROWS = N_QKVO + BF16_ROWS
OFF_U, OFF_Z = 0, D_S
WUZ_ROWS = 2 * D_S
AUG = DV + LANES
ACT_Q, ACT_K, ACT_V = 0, QK_W, 2 * QK_W
ACT_O = ACT_V + N_HEADS * AUG
ACT_U = ACT_O + D_M
ACT_Z = ACT_U + D_S
D_ACT = ACT_Z + D_S
OP_CHUNK = 512
MIX_ORDER = ("op:0", "op:1", "op:2", "op:3", "free", "sgu:0", "state", "sgu:1", "intra", "sgu:2",
             "sgu:3")
SCAN_B, SCAN_R, SCAN_CM = 0, N_GATES, 2 * N_GATES
SCAN_ROWS = 3 * N_GATES

VMEM_LIMIT = 56 * 1024 * 1024


def _resident(shape):
    nd = len(shape)
    return pl.BlockSpec(shape, lambda *_: (0,) * nd, pipeline_mode=pl.Buffered(1))


def _rms_scale(x):
    return lax.rsqrt(jnp.mean(x * x, axis=-1, keepdims=True) + EPS)


def _block_scan(x, op, fill, blk):
    lane = lax.broadcasted_iota(jnp.int32, x.shape, 1) & (blk - 1)
    shift = 1
    while shift < blk:
        shifted = pltpu.roll(x, shift, axis=1)
        x = op(x, jnp.where(lane >= shift, shifted, fill))
        shift *= 2
    return x


def _inproj_kernel(*refs, scan_blk, emit_zs, cast_weights):
    x_ref, g1_ref, wa_ref, wuz_ref, gsgu_ref, bg_ref = refs[0:6]
    refs = refs[6:]
    if cast_weights:
        wsrc = refs[0:3]
        refs = refs[3:]
    act_ref, scan_ref = refs[0:2]
    refs = refs[2:]
    if emit_zs:
        zs_ref = refs[0]
        refs = refs[1:]
    if cast_weights:
        for src, dst in zip(wsrc, refs):
            dst[...] = src[...].astype(BF16)

    x = x_ref[...]
    h = (x * _rms_scale(x) * g1_ref[...]).astype(BF16)

    def proj(w_ref, off, width):
        return lax.dot_general(h, w_ref[off:off + width, :], (((1,), (1,)), ((), ())),
                               preferred_element_type=F32)

    gates_t = _pad_lanes(proj(wa_ref, OFF_G, BF16_ROWS), LANES).T
    g8 = gates_t[0:N_GATES, :] + bg_ref[:, 0:1]
    b8 = pltpu.roll(_block_scan(jax.nn.log_sigmoid(g8), jnp.add, 0.0, scan_blk), N_HEADS, axis=0)
    r8 = g8 - b8
    scan_ref[SCAN_B:SCAN_B + N_GATES, :] = b8
    scan_ref[SCAN_R:SCAN_R + N_GATES, :] = r8
    scan_ref[SCAN_CM:SCAN_CM + N_GATES, :] = _block_scan(r8, jnp.maximum, -jnp.inf, scan_blk)

    def group_z():
        zg = jax.nn.gelu(proj(wuz_ref, OFF_Z, D_S))
        zs = zg * _rms_scale(zg) * gsgu_ref[...]
        act_ref[:, ACT_Z:ACT_Z + D_S] = zs.astype(BF16)
        if emit_zs:
            zs_ref[...] = zs

    def group_u():
        act_ref[:, ACT_U:ACT_U + D_S] = jax.nn.gelu(proj(wuz_ref, OFF_U, D_S)).astype(BF16)

    def group_o():
        act_ref[:, ACT_O:ACT_O + D_M] = jax.nn.sigmoid(proj(wa_ref, OFF_O, D_M)).astype(BF16)

    def group_k():
        act_ref[:, ACT_K:ACT_K + QK_W] = (proj(wa_ref, OFF_K, QK_W) * (DQK ** -0.5)).astype(BF16)

    def group_q():
        act_ref[:, ACT_Q:ACT_Q + QK_W] = proj(wa_ref, OFF_Q, QK_W).astype(BF16)

    def group_v():
        v = proj(wa_ref, OFF_V, D_M).astype(BF16)
        ones = jnp.ones((v.shape[0], LANES), BF16)
        for hd in range(N_HEADS):
            act_ref[:, ACT_V + hd * AUG:ACT_V + hd * AUG + DV] = v[:, hd * DV:(hd + 1) * DV]
            act_ref[:, ACT_V + hd * AUG + DV:ACT_V + (hd + 1) * AUG] = ones

    for group in (group_z, group_v, group_u, group_k, group_o, group_q):
        group()


def _inproj(x2d, g1, w_parts, g_sgu, b_gate, tm, scan_blk, emit_zs, cast_srcs=None):
    t = x2d.shape[0]
    assert t % tm == 0
    n = t // tm
    row = lambda width: pl.BlockSpec((tm, width), lambda i: (i, 0))
    in_specs = [row(D_MODEL), _resident((1, D_MODEL)), _resident((WA_ROWS, D_MODEL)),
                _resident((WUZ_ROWS, D_MODEL)), _resident((1, D_S)), _resident((N_GATES, LANES))]
    args = [x2d, g1, *w_parts, g_sgu, b_gate]
    out_specs = [row(D_ACT), pl.BlockSpec((SCAN_ROWS, tm), lambda i: (0, i))]
    out_shape = [jax.ShapeDtypeStruct((t, D_ACT), BF16), jax.ShapeDtypeStruct((SCAN_ROWS, t), F32)]
    if emit_zs:
        out_specs.append(row(D_S))
        out_shape.append(jax.ShapeDtypeStruct((t, D_S), F32))
    if cast_srcs is not None:
        for w in cast_srcs:
            assert w.shape[0] % n == 0
            slab = pl.BlockSpec((w.shape[0] // n, w.shape[1]), lambda i: (i, 0))
            in_specs.append(slab)
            args.append(w)
            out_specs.append(slab)
            out_shape.append(jax.ShapeDtypeStruct(w.shape, BF16))
    return pl.pallas_call(
        functools.partial(_inproj_kernel, scan_blk=scan_blk, emit_zs=emit_zs,
                          cast_weights=cast_srcs is not None),
        grid=(n,),
        in_specs=in_specs,
        out_specs=out_specs,
        out_shape=out_shape,
        compiler_params=pltpu.CompilerParams(dimension_semantics=("arbitrary",),
                                             vmem_limit_bytes=VMEM_LIMIT),
        name="inproj",
    )(*args)


def _col_bcast(row):
    return jnp.broadcast_to(row, (BLK, BLK)).T


def _pad_rows(a, rows):
    if a.shape[0] == rows:
        return a
    return jnp.concatenate([a, jnp.zeros((rows - a.shape[0],) + a.shape[1:], a.dtype)], axis=0)


def _pad_lanes(a, lanes):
    if a.shape[1] == lanes:
        return a
    return jnp.concatenate([a, jnp.zeros((a.shape[0], lanes - a.shape[1]), a.dtype)], axis=1)


def _mix_kernel(*refs, ts, lc, nj, n_tiles, has_state):
    (act_ref, scan_ref, x_ref, gmh_ref, wsp_ref, bsp_ref, wout_ref, g2_ref) = refs[0:8]
    refs = refs[8:]
    if has_state:
        c0_ref, n0_ref, m0_ref = refs[0:3]
        refs = refs[3:]
    (x1_ref, xn_ref, cout_ref, nout_ref, mout_ref,
     caug_scr, m_scr, ya_scr, yb_scr, wtri_scr, bcol_scr) = refs
    g = pl.program_id(0)
    j = jnp.minimum(g, n_tiles - 1) % nj
    tp = max(ts, BLK)
    nblk = tp // BLK
    valid = min(ts, BLK)

    @pl.when(g == 0)
    def _prepare():
        yb_scr[...] = jnp.zeros(yb_scr.shape, BF16)
        row_i = lax.broadcasted_iota(jnp.int32, (SGU_CHUNK, SGU_CHUNK), 0)
        col_i = lax.broadcasted_iota(jnp.int32, (SGU_CHUNK, SGU_CHUNK), 1)
        for gi in range(N_GROUPS):
            wtri_scr[gi] = jnp.where(col_i <= row_i, wsp_ref[gi], 0.0).astype(BF16)
            b_col = _col_bcast(bsp_ref[gi:gi + 1, :])
            bcol_scr[gi] = jnp.concatenate([b_col, b_col], axis=1)

    @pl.when(j == 0)
    def _init():
        if has_state:
            caug_scr[:, :, 0:DV] = c0_ref[0]
            for h in range(N_HEADS):
                caug_scr[h, :, DV:AUG] = _col_bcast(n0_ref[0, h:h + 1, :])
            m_scr[...] = m0_ref[0]
        else:
            caug_scr[...] = jnp.zeros(caug_scr.shape, F32)
            m_scr[...] = jnp.zeros(m_scr.shape, F32)

    def body(y_cur, y_prev):
        sumsq = []

        def out_proj_chunk(n):
            cols = slice(n * OP_CHUNK, (n + 1) * OP_CHUNK)
            part = x_ref[:, cols] + jnp.dot(y_prev[...], wout_ref[:, cols],
                                            preferred_element_type=F32)
            x1_ref[:, cols] = part
            sumsq.append(jnp.sum(part * part, axis=-1, keepdims=True))

        def sgu_group(gi):
            w_s = wtri_scr[gi, 0:lc, 0:lc]
            b_col = bcol_scr[gi, 0:lc, :]
            for c in range(ts // lc):
                rows = slice(c * lc, (c + 1) * lc)
                z = act_ref[rows, ACT_Z + gi * GS_W:ACT_Z + (gi + 1) * GS_W]
                u = act_ref[rows, ACT_U + gi * GS_W:ACT_U + (gi + 1) * GS_W].astype(F32)
                mixed = jnp.dot(w_s, z, preferred_element_type=F32) + b_col
                y_cur[rows, D_M + gi * GS_W:D_M + (gi + 1) * GS_W] = (u * mixed).astype(BF16)

        scan = scan_ref[...] if scan_ref.ndim == 2 else scan_ref[0]
        scan = _pad_lanes(scan, tp)
        b8 = scan[SCAN_B:SCAN_B + N_GATES]
        r8 = scan[SCAN_R:SCAN_R + N_GATES]
        cm8 = scan[SCAN_CM:SCAN_CM + N_GATES]

        row_i = lax.broadcasted_iota(jnp.int32, (BLK, BLK), 0)
        col_i = lax.broadcasted_iota(jnp.int32, (BLK, BLK), 1)
        causal = col_i <= row_i
        lane8 = lax.broadcasted_iota(jnp.int32, (N_GATES, BLK), 1)

        def load(c, off, width, h):
            rows_v = slice(c * BLK, c * BLK + valid)
            return _pad_rows(act_ref[rows_v, off + h * width:off + (h + 1) * width], BLK)

        def load_v_aug(c, h):
            return load(c, ACT_V, AUG, h)

        gate = []
        m_prev = m_scr[:, 0:1]
        for c in range(nblk):
            lo = c * BLK
            r_c = r8[:, lo:lo + BLK]
            b_c = b8[:, lo:lo + BLK]
            g_c = jnp.maximum(cm8[:, lo:lo + BLK], m_prev)
            g_last = g_c[:, valid - 1:valid]
            gate.append(dict(
                r=r_c, g=g_c, mt=g_c + b_c, m0=m_prev,
                ws=jnp.where(lane8 < valid, jnp.exp(r_c - g_last), 0.0),
                decay=jnp.exp(m_prev - g_last)))
            m_prev = g_last + b_c[:, valid - 1:valid]
        m_scr[...] = jnp.broadcast_to(m_prev, m_scr.shape)

        heads_blocks = [(c, h) for c in range(nblk) for h in range(N_HEADS)]

        s_all, upd_all, inter_all, intra_all, gcol_all = {}, {}, {}, {}, {}

        def stage_free():
            for c, h in heads_blocks:
                k = load(c, ACT_K, DQK, h)
                s_all[c, h] = lax.dot_general(load(c, ACT_Q, DQK, h), k, (((1,), (1,)), ((), ())),
                                              preferred_element_type=F32)
                ws_row = jnp.broadcast_to(gate[c]["ws"][h:h + 1, :], (BLK, BLK))
                kts = (k.astype(F32).T * ws_row).astype(BF16)
                upd_all[c, h] = jnp.dot(kts, load_v_aug(c, h), preferred_element_type=F32)

        def stage_state():
            for h in range(N_HEADS):
                caug = caug_scr[h]
                for c in range(nblk):
                    inter_all[c, h] = jnp.dot(load(c, ACT_Q, DQK, h), caug.astype(BF16),
                                              preferred_element_type=F32)
                    caug = gate[c]["decay"][h:h + 1, :] * caug + upd_all[c, h]
                caug_scr[h] = caug

        def stage_intra():
            for c, h in heads_blocks:
                g_col = _col_bcast(gate[c]["g"][h:h + 1, :])
                arg = jnp.broadcast_to(gate[c]["r"][h:h + 1, :], (BLK, BLK)) - g_col
                p = jnp.exp(jnp.where(causal, arg, -jnp.inf))
                sw = (s_all[c, h] * p).astype(BF16)
                intra_all[c, h] = jnp.dot(sw, load_v_aug(c, h), preferred_element_type=F32)
                gcol_all[c, h] = g_col

        steps = {"free": stage_free, "state": stage_state, "intra": stage_intra}
        for name in MIX_ORDER:
            kind, _, arg = name.partition(":")
            if kind == "op":
                out_proj_chunk(int(arg))
            elif kind == "sgu":
                sgu_group(int(arg))
            else:
                steps[kind]()
        assert len(sumsq) * OP_CHUNK == D_MODEL

        for c, h in heads_blocks:
            rows_v = slice(c * BLK, c * BLK + valid)
            w_col = jnp.exp(gate[c]["m0"][h:h + 1, :] - gcol_all[c, h])
            tot = jnp.concatenate([w_col, w_col, w_col], axis=1) * inter_all[c, h] + intra_all[c, h]
            num = tot[:, 0:DV]
            den = tot[:, DV:AUG]
            dn = jnp.maximum(jnp.abs(den), jnp.exp(-_col_bcast(gate[c]["mt"][h:h + 1, :])))
            hh = num / jnp.concatenate([dn, dn], axis=1)
            hn = hh * _rms_scale(hh) * gmh_ref[:, h * DV:(h + 1) * DV]
            og = act_ref[rows_v, ACT_O + h * DV:ACT_O + (h + 1) * DV].astype(F32)
            y_cur[rows_v, h * DV:(h + 1) * DV] = (hn[0:valid] * og).astype(BF16)

        ms = functools.reduce(jnp.add, sumsq) * (1.0 / D_MODEL)
        xn_ref[...] = (x1_ref[...] * lax.rsqrt(ms + EPS) * g2_ref[...]).astype(BF16)

    @pl.when(g % 2 == 0)
    def _even():
        body(ya_scr, yb_scr)

    @pl.when(g % 2 == 1)
    def _odd():
        body(yb_scr, ya_scr)

    @pl.when((j == nj - 1) & (g < n_tiles))
    def _final():
        cout_ref[0] = caug_scr[:, :, 0:DV]
        for h in range(N_HEADS):
            nout_ref[0, h:h + 1, :] = caug_scr[h, :, DV:AUG].T[0:1, :]
        mout_ref[0] = m_scr[...]


def _mix(act, scan, x2d, g_mh, w_sp, b_sp, w_out_b, g2, state, batch, seq, ts):
    lc = min(seq, SGU_CHUNK)
    assert seq % ts == 0 and ts % lc == 0 and (ts % BLK == 0 or ts < BLK)
    nj = seq // ts
    n_tiles = batch * nj
    has_state = state is not None
    cur = lambda g: jnp.minimum(g, n_tiles - 1)
    prev = lambda g: jnp.maximum(g - 1, 0)
    row_cur = lambda width: pl.BlockSpec((ts, width), lambda g: (cur(g), 0))
    row_prev = lambda width: pl.BlockSpec((ts, width), lambda g: (prev(g), 0))
    per_b = lambda shape: pl.BlockSpec((1,) + shape, lambda g: (cur(g) // nj,) + (0,) * len(shape))
    if scan.ndim == 2:
        scan_spec = pl.BlockSpec((SCAN_ROWS, ts), lambda g: (0, cur(g)))
    else:
        scan_spec = pl.BlockSpec((1, SCAN_ROWS, ts), lambda g: (cur(g), 0, 0))
    in_specs = [row_cur(D_ACT), scan_spec, row_prev(D_MODEL), _resident((1, D_M)),
                _resident((N_GROUPS, SGU_CHUNK, SGU_CHUNK)), _resident((N_GROUPS, SGU_CHUNK)),
                _resident((D_MODEL, D_MODEL)), _resident((1, D_MODEL))]
    args = [act, scan, x2d, g_mh, w_sp, b_sp, w_out_b, g2]
    if has_state:
        in_specs += [per_b((N_HEADS, DQK, DV)), per_b((N_HEADS, DQK)), per_b((N_GATES, LANES))]
        args += list(state)
    return pl.pallas_call(
        functools.partial(_mix_kernel, ts=ts, lc=lc, nj=nj, n_tiles=n_tiles, has_state=has_state),
        grid=(n_tiles + 1,),
        in_specs=in_specs,
        out_specs=[row_prev(D_MODEL), row_prev(D_MODEL), per_b((N_HEADS, DQK, DV)),
                   per_b((N_HEADS, DQK)), per_b((N_GATES, LANES))],
        out_shape=[jax.ShapeDtypeStruct((batch * seq, D_MODEL), F32),
                   jax.ShapeDtypeStruct((batch * seq, D_MODEL), BF16),
                   jax.ShapeDtypeStruct((batch, N_HEADS, DQK, DV), F32),
                   jax.ShapeDtypeStruct((batch, N_HEADS, DQK), F32),
                   jax.ShapeDtypeStruct((batch, N_GATES, LANES), F32)],
        scratch_shapes=[pltpu.VMEM((N_HEADS, DQK, AUG), F32),
                        pltpu.VMEM((N_GATES, LANES), F32),
                        pltpu.VMEM((ts, D_MODEL), BF16),
                        pltpu.VMEM((ts, D_MODEL), BF16),
                        pltpu.VMEM((N_GROUPS, SGU_CHUNK, SGU_CHUNK), BF16),
                        pltpu.VMEM((N_GROUPS, SGU_CHUNK, GS_W), F32)],
        compiler_params=pltpu.CompilerParams(dimension_semantics=("arbitrary",),
                                             vmem_limit_bytes=VMEM_LIMIT),
        name="mix",
    )(*args)


def _ffn_kernel(xn_ref, x1c_ref, w1_ref, w2_ref, gf_ref, y_ref):
    kf = pl.program_id(1)
    rows = x1c_ref.shape[0]

    def delta():
        hid = jnp.dot(xn_ref[...], w1_ref[...], preferred_element_type=F32)
        hid = jnp.square(jnp.maximum(hid, 0.0)).astype(BF16)
        return jnp.dot(hid, w2_ref[...], preferred_element_type=F32)

    last = pl.num_programs(1) - 1
    slab = pl.ds(pl.multiple_of(kf * rows, rows), rows)

    @pl.when(kf == 0)
    def _first():
        y_ref[...] = delta()
        y_ref[slab, :] += x1c_ref[...]

    @pl.when((kf != 0) & (kf != last))
    def _middle():
        y_ref[...] += delta()
        y_ref[slab, :] += x1c_ref[...]

    @pl.when(kf == last)
    def _last():
        y_ref[slab, :] += x1c_ref[...]
        y = y_ref[...] + delta()
        y_ref[...] = y * _rms_scale(y) * gf_ref[...]


def _ffn(xn, x1, w1_b, w2_b, g_final, tm, kf):
    t = x1.shape[0]
    assert t % tm == 0 and D_FF % kf == 0
    nk = D_FF // kf
    assert tm % nk == 0 and nk >= 2
    return pl.pallas_call(
        _ffn_kernel,
        grid=(t // tm, nk),
        in_specs=[pl.BlockSpec((tm, D_MODEL), lambda i, k: (i, 0)),
                  pl.BlockSpec((tm // nk, D_MODEL), lambda i, k: (i * nk + k, 0)),
                  pl.BlockSpec((D_MODEL, kf), lambda i, k: (0, k)),
                  pl.BlockSpec((kf, D_MODEL), lambda i, k: (k, 0)),
                  _resident((1, D_MODEL))],
        out_specs=pl.BlockSpec((tm, D_MODEL), lambda i, k: (i, 0)),
        out_shape=jax.ShapeDtypeStruct((t, D_MODEL), F32),
        compiler_params=pltpu.CompilerParams(dimension_semantics=("arbitrary", "arbitrary"),
                                             vmem_limit_bytes=VMEM_LIMIT),
        name="ffn",
    )(xn, x1, w1_b, w2_b, g_final)


def kernel(x_prompt, x_sample, state_mlstm_C, state_mlstm_n, state_mlstm_m, w_in, b_gate, g_mh,
           g_sgu, w_sp, b_sp, w_out, g_norm1, g_norm2, w_ff1, w_ff2, g_final):
    depth = w_in.shape[0]
    assert depth == 1, "single-layer step"
    l = 0
    d_in = w_in.shape[2]

    w_t = jnp.swapaxes(w_in, 1, 2)

    def w_rows(lo, hi):
        return lax.slice(w_t, (l, lo, 0), (l + 1, hi, D_MODEL)).reshape(hi - lo, D_MODEL).astype(BF16)

    assert d_in == N_QKVO + N_GATES + 2 * D_S
    w_r = (w_rows(0, N_QKVO + BF16_ROWS), w_rows(N_QKVO + N_GATES, d_in))
    bg = jnp.broadcast_to(b_gate[l][:, None], (N_GATES, LANES))
    g1, g2, gf = g_norm1[l][None, :], g_norm2[l][None, :], g_final[None, :]
    gmh, gsgu = g_mh[l][None, :], g_sgu[l][None, :]

    bp, sp, _ = x_prompt.shape
    xp2d = x_prompt.reshape(bp * sp, D_MODEL)
    act, scan, w1_b, w2_b, w_out_b = _inproj(xp2d, g1, w_r, gsgu, bg, tm=512, scan_blk=BLK,
                                             emit_zs=False, cast_srcs=(w_ff1[l], w_ff2[l], w_out[l]))
    x1, xn, c_p, n_p, m_p = _mix(act, scan, xp2d, gmh, w_sp[l], b_sp[l], w_out_b, g2, None,
                                 bp, sp, ts=512)
    yp = _ffn(xn, x1, w1_b, w2_b, gf, tm=1024, kf=1024)

    bs, ss, _ = x_sample.shape
    xs2d = x_sample.reshape(bs * ss, D_MODEL)
    act_s, scan_s, zs_s = _inproj(xs2d, g1, w_r, gsgu, bg, tm=bs * ss, scan_blk=ss, emit_zs=True)
    scan_s = scan_s.reshape(SCAN_ROWS, bs, ss).transpose(1, 0, 2)
    m0 = jnp.broadcast_to(
        jnp.pad(state_mlstm_m[l], ((0, 0), (0, N_GATES - N_HEADS)))[:, :, None],
        (bs, N_GATES, LANES))
    x1s, xns, c_s, n_s, m_s = _mix(act_s, scan_s, xs2d, gmh, w_sp[l], b_sp[l], w_out_b, g2,
                                   (state_mlstm_C[l], state_mlstm_n[l], m0), bs, ss, ts=ss)
    ys = _ffn(xns, x1s, w1_b, w2_b, gf, tm=bs * ss, kf=1024)

    return (yp.reshape(bp, sp, D_MODEL), ys.reshape(bs, ss, D_MODEL),
            c_p[None], n_p[None], m_p[:, 0:N_HEADS, 0][None],
            c_s[None], n_s[None], m_s[:, 0:N_HEADS, 0][None],
            zs_s.reshape(bs, ss, D_S)[None])
```

```python
import functools

import jax
import jax.numpy as jnp
from jax import lax
from jax.experimental import pallas as pl
from jax.experimental.pallas import tpu as pltpu

F32 = jnp.float32
BF16 = jnp.bfloat16

D_MODEL = 2048
N_HEADS = 4
DQK = 128
DV = 256
QK_W = N_HEADS * DQK
D_M = N_HEADS * DV
N_GROUPS = 4
GS_W = 256
D_S = N_GROUPS * GS_W
D_FF = 4 * D_MODEL
SGU_CHUNK = 128
N_GATES = 2 * N_HEADS
EPS = 1e-6

LANES = 128
BLK = 128
GATE_PAD = LANES
N_QKVO = 2 * QK_W + 2 * D_M
OFF_Q, OFF_K, OFF_V = 0, QK_W, 2 * QK_W
OFF_O = OFF_V + D_M
OFF_U, OFF_Z = 0, D_S
AUG = DV + LANES
ACT_Q, ACT_K, ACT_V = 0, QK_W, 2 * QK_W
ACT_O = ACT_V + N_HEADS * AUG
ACT_U = ACT_O + D_M
ACT_Z = ACT_U + D_S
D_ACT = ACT_Z + D_S
OP_CHUNK = 512
MIX_ORDER = ("op:0", "op:1", "op:2", "op:3", "free", "sgu:0", "state", "sgu:1", "intra", "sgu:2",
             "sgu:3")
SCAN_B, SCAN_R, SCAN_CM = 0, N_GATES, 2 * N_GATES
SCAN_ROWS = 3 * N_GATES

VMEM_LIMIT = 56 * 1024 * 1024


def _resident(shape):
    nd = len(shape)
    return pl.BlockSpec(shape, lambda *_: (0,) * nd, pipeline_mode=pl.Buffered(1))


def _rms_scale(x):
    return lax.rsqrt(jnp.mean(x * x, axis=-1, keepdims=True) + EPS)


_LOG2E = 1.4426950408889634
_GELU_C = 0.7978845608028654
_GELU_A = 0.044715


def _gelu(x):
    k1 = -2.0 * _GELU_C * _LOG2E
    k2 = k1 * _GELU_A
    return x / (1.0 + jnp.exp2(x * (k1 + k2 * (x * x))))


def _sigmoid(x):
    return 1.0 / (1.0 + jnp.exp2(x * (-_LOG2E)))


def _block_scan(x, op, fill, blk):
    lane = lax.broadcasted_iota(jnp.int32, x.shape, 1) & (blk - 1)
    shift = 1
    while shift < blk:
        shifted = pltpu.roll(x, shift, axis=1)
        x = op(x, jnp.where(lane >= shift, shifted, fill))
        shift *= 2
    return x


def _inproj_kernel(*refs, scan_blk, emit_zs, cast_weights):
    x_ref, g1_ref, wa_ref, wuz_ref, wg_ref, gsgu_ref, bg_ref = refs[0:7]
    refs = refs[7:]
    if cast_weights:
        wsrc = refs[0:3]
        refs = refs[3:]
    act_ref, scan_ref = refs[0:2]
    refs = refs[2:]
    if emit_zs:
        zs_ref = refs[0]
        refs = refs[1:]
    if cast_weights:
        for src, dst in zip(wsrc, refs):
            dst[...] = src[...].astype(BF16)

    x = x_ref[...]
    h = (x * _rms_scale(x) * g1_ref[...]).astype(BF16)

    def proj(w_ref, off, width):
        return jnp.dot(h, w_ref[:, off:off + width], preferred_element_type=F32)

    g8 = proj(wg_ref, 0, GATE_PAD).T[0:N_GATES, :] + bg_ref[:, 0:1]
    b8 = pltpu.roll(_block_scan(jax.nn.log_sigmoid(g8), jnp.add, 0.0, scan_blk), N_HEADS, axis=0)
    r8 = g8 - b8
    scan_ref[SCAN_B:SCAN_B + N_GATES, :] = b8
    scan_ref[SCAN_R:SCAN_R + N_GATES, :] = r8
    scan_ref[SCAN_CM:SCAN_CM + N_GATES, :] = _block_scan(r8, jnp.maximum, -jnp.inf, scan_blk)

    def group_z():
        zg = _gelu(proj(wuz_ref, OFF_Z, D_S))
        zs = zg * _rms_scale(zg) * gsgu_ref[...]
        act_ref[:, ACT_Z:ACT_Z + D_S] = zs.astype(BF16)
        if emit_zs:
            zs_ref[...] = zs

    def group_u():
        act_ref[:, ACT_U:ACT_U + D_S] = _gelu(proj(wuz_ref, OFF_U, D_S)).astype(BF16)

    def group_o():
        act_ref[:, ACT_O:ACT_O + D_M] = _sigmoid(proj(wa_ref, OFF_O, D_M)).astype(BF16)

    def group_k():
        act_ref[:, ACT_K:ACT_K + QK_W] = (proj(wa_ref, OFF_K, QK_W) * (DQK ** -0.5)).astype(BF16)

    def group_q():
        act_ref[:, ACT_Q:ACT_Q + QK_W] = proj(wa_ref, OFF_Q, QK_W).astype(BF16)

    def group_v():
        v = proj(wa_ref, OFF_V, D_M).astype(BF16)
        ones = jnp.ones((v.shape[0], LANES), BF16)
        for hd in range(N_HEADS):
            act_ref[:, ACT_V + hd * AUG:ACT_V + hd * AUG + DV] = v[:, hd * DV:(hd + 1) * DV]
            act_ref[:, ACT_V + hd * AUG + DV:ACT_V + (hd + 1) * AUG] = ones

    for group in (group_z, group_v, group_u, group_k, group_o, group_q):
        group()


def _inproj(x2d, g1, w_parts, g_sgu, b_gate, tm, scan_blk, emit_zs, cast_srcs=None):
    t = x2d.shape[0]
    assert t % tm == 0
    n = t // tm
    row = lambda width: pl.BlockSpec((tm, width), lambda i: (i, 0))
    in_specs = [row(D_MODEL), _resident((1, D_MODEL)), _resident((D_MODEL, N_QKVO)),
                _resident((D_MODEL, 2 * D_S)), _resident((D_MODEL, GATE_PAD)),
                _resident((1, D_S)), _resident((N_GATES, LANES))]
    args = [x2d, g1, *w_parts, g_sgu, b_gate]
    out_specs = [row(D_ACT), pl.BlockSpec((SCAN_ROWS, tm), lambda i: (0, i))]
    out_shape = [jax.ShapeDtypeStruct((t, D_ACT), BF16), jax.ShapeDtypeStruct((SCAN_ROWS, t), F32)]
    if emit_zs:
        out_specs.append(row(D_S))
        out_shape.append(jax.ShapeDtypeStruct((t, D_S), F32))
    if cast_srcs is not None:
        for w in cast_srcs:
            assert w.shape[0] % n == 0
            slab = pl.BlockSpec((w.shape[0] // n, w.shape[1]), lambda i: (i, 0))
            in_specs.append(slab)
            args.append(w)
            out_specs.append(slab)
            out_shape.append(jax.ShapeDtypeStruct(w.shape, BF16))
    return pl.pallas_call(
        functools.partial(_inproj_kernel, scan_blk=scan_blk, emit_zs=emit_zs,
                          cast_weights=cast_srcs is not None),
        grid=(n,),
        in_specs=in_specs,
        out_specs=out_specs,
        out_shape=out_shape,
        compiler_params=pltpu.CompilerParams(dimension_semantics=("arbitrary",),
                                             vmem_limit_bytes=VMEM_LIMIT),
        name="inproj",
    )(*args)


def _col_bcast(row):
    return jnp.broadcast_to(row, (BLK, BLK)).T


def _pad_rows(a, rows):
    if a.shape[0] == rows:
        return a
    return jnp.concatenate([a, jnp.zeros((rows - a.shape[0],) + a.shape[1:], a.dtype)], axis=0)


def _pad_lanes(a, lanes):
    if a.shape[1] == lanes:
        return a
    return jnp.concatenate([a, jnp.zeros((a.shape[0], lanes - a.shape[1]), a.dtype)], axis=1)


def _mix_kernel(*refs, ts, lc, nj, n_tiles, has_state):
    (act_ref, scan_ref, x_ref, gmh_ref, wsp_ref, bsp_ref, wout_ref, g2_ref) = refs[0:8]
    refs = refs[8:]
    if has_state:
        c0_ref, n0_ref, m0_ref = refs[0:3]
        refs = refs[3:]
    (x1_ref, xn_ref, cout_ref, nout_ref, mout_ref,
     caug_scr, m_scr, ya_scr, yb_scr, wtri_scr, bcol_scr) = refs
    g = pl.program_id(0)
    j = jnp.minimum(g, n_tiles - 1) % nj
    tp = max(ts, BLK)
    nblk = tp // BLK
    valid = min(ts, BLK)

    @pl.when(g == 0)
    def _prepare():
        yb_scr[...] = jnp.zeros(yb_scr.shape, BF16)
        row_i = lax.broadcasted_iota(jnp.int32, (SGU_CHUNK, SGU_CHUNK), 0)
        col_i = lax.broadcasted_iota(jnp.int32, (SGU_CHUNK, SGU_CHUNK), 1)
        for gi in range(N_GROUPS):
            wtri_scr[gi] = jnp.where(col_i <= row_i, wsp_ref[gi], 0.0).astype(BF16)
            b_col = _col_bcast(bsp_ref[gi:gi + 1, :])
            bcol_scr[gi] = jnp.concatenate([b_col, b_col], axis=1)

    @pl.when(j == 0)
    def _init():
        if has_state:
            caug_scr[:, :, 0:DV] = c0_ref[0]
            for h in range(N_HEADS):
                caug_scr[h, :, DV:AUG] = _col_bcast(n0_ref[0, h:h + 1, :])
            m_scr[...] = m0_ref[0]
        else:
            caug_scr[...] = jnp.zeros(caug_scr.shape, F32)
            m_scr[...] = jnp.zeros(m_scr.shape, F32)

    def body(y_cur, y_prev):
        sumsq = []

        def out_proj_chunk(n):
            cols = slice(n * OP_CHUNK, (n + 1) * OP_CHUNK)
            part = x_ref[:, cols] + jnp.dot(y_prev[...], wout_ref[:, cols],
                                            preferred_element_type=F32)
            x1_ref[:, cols] = part
            sumsq.append(jnp.sum(part * part, axis=-1, keepdims=True))

        def sgu_group(gi):
            w_s = wtri_scr[gi, 0:lc, 0:lc]
            b_col = bcol_scr[gi, 0:lc, :]
            for c in range(ts // lc):
                rows = slice(c * lc, (c + 1) * lc)
                z = act_ref[rows, ACT_Z + gi * GS_W:ACT_Z + (gi + 1) * GS_W]
                u = act_ref[rows, ACT_U + gi * GS_W:ACT_U + (gi + 1) * GS_W].astype(F32)
                mixed = jnp.dot(w_s, z, preferred_element_type=F32) + b_col
                y_cur[rows, D_M + gi * GS_W:D_M + (gi + 1) * GS_W] = (u * mixed).astype(BF16)

        scan = scan_ref[...] if scan_ref.ndim == 2 else scan_ref[0]
        scan = _pad_lanes(scan, tp)
        b8 = scan[SCAN_B:SCAN_B + N_GATES]
        r8 = scan[SCAN_R:SCAN_R + N_GATES]
        cm8 = scan[SCAN_CM:SCAN_CM + N_GATES]

        row_i = lax.broadcasted_iota(jnp.int32, (BLK, BLK), 0)
        col_i = lax.broadcasted_iota(jnp.int32, (BLK, BLK), 1)
        causal = col_i <= row_i
        lane8 = lax.broadcasted_iota(jnp.int32, (N_GATES, BLK), 1)

        def load(c, off, width, h):
            rows_v = slice(c * BLK, c * BLK + valid)
            return _pad_rows(act_ref[rows_v, off + h * width:off + (h + 1) * width], BLK)

        def load_v_aug(c, h):
            return load(c, ACT_V, AUG, h)

        gate = []
        m_prev = m_scr[:, 0:1]
        for c in range(nblk):
            lo = c * BLK
            r_c = r8[:, lo:lo + BLK]
            b_c = b8[:, lo:lo + BLK]
            g_c = jnp.maximum(cm8[:, lo:lo + BLK], m_prev)
            g_last = g_c[:, valid - 1:valid]
            gate.append(dict(
                r=r_c, g=g_c, mt=g_c + b_c, m0=m_prev,
                ws=jnp.where(lane8 < valid, jnp.exp(r_c - g_last), 0.0),
                decay=jnp.exp(m_prev - g_last)))
            m_prev = g_last + b_c[:, valid - 1:valid]
        m_scr[...] = jnp.broadcast_to(m_prev, m_scr.shape)

        heads_blocks = [(c, h) for c in range(nblk) for h in range(N_HEADS)]

        s_all, upd_all, inter_all, intra_all, gcol_all = {}, {}, {}, {}, {}

        def stage_free():
            for c, h in heads_blocks:
                k = load(c, ACT_K, DQK, h)
                s_all[c, h] = lax.dot_general(load(c, ACT_Q, DQK, h), k, (((1,), (1,)), ((), ())),
                                              preferred_element_type=F32)
                ws_row = jnp.broadcast_to(gate[c]["ws"][h:h + 1, :], (BLK, BLK))
                kts = (k.astype(F32).T * ws_row).astype(BF16)
                upd_all[c, h] = jnp.dot(kts, load_v_aug(c, h), preferred_element_type=F32)

        def stage_state():
            for h in range(N_HEADS):
                caug = caug_scr[h]
                for c in range(nblk):
                    inter_all[c, h] = jnp.dot(load(c, ACT_Q, DQK, h), caug.astype(BF16),
                                              preferred_element_type=F32)
                    caug = gate[c]["decay"][h:h + 1, :] * caug + upd_all[c, h]
                caug_scr[h] = caug

        def stage_intra():
            for c, h in heads_blocks:
                g_col = _col_bcast(gate[c]["g"][h:h + 1, :])
                arg = jnp.broadcast_to(gate[c]["r"][h:h + 1, :], (BLK, BLK)) - g_col
                p = jnp.exp(jnp.where(causal, arg, -jnp.inf))
                sw = (s_all[c, h] * p).astype(BF16)
                intra_all[c, h] = jnp.dot(sw, load_v_aug(c, h), preferred_element_type=F32)
                gcol_all[c, h] = g_col

        steps = {"free": stage_free, "state": stage_state, "intra": stage_intra}
        for name in MIX_ORDER:
            kind, _, arg = name.partition(":")
            if kind == "op":
                out_proj_chunk(int(arg))
            elif kind == "sgu":
                sgu_group(int(arg))
            else:
                steps[kind]()
        assert len(sumsq) * OP_CHUNK == D_MODEL

        for c, h in heads_blocks:
            rows_v = slice(c * BLK, c * BLK + valid)
            w_col = jnp.exp(gate[c]["m0"][h:h + 1, :] - gcol_all[c, h])
            tot = jnp.concatenate([w_col, w_col, w_col], axis=1) * inter_all[c, h] + intra_all[c, h]
            num = tot[:, 0:DV]
            den = tot[:, DV:AUG]
            dn = jnp.maximum(jnp.abs(den), jnp.exp(-_col_bcast(gate[c]["mt"][h:h + 1, :])))
            hh = num / jnp.concatenate([dn, dn], axis=1)
            hn = hh * _rms_scale(hh) * gmh_ref[:, h * DV:(h + 1) * DV]
            og = act_ref[rows_v, ACT_O + h * DV:ACT_O + (h + 1) * DV].astype(F32)
            y_cur[rows_v, h * DV:(h + 1) * DV] = (hn[0:valid] * og).astype(BF16)

        ms = functools.reduce(jnp.add, sumsq) * (1.0 / D_MODEL)
        xn_ref[...] = (x1_ref[...] * lax.rsqrt(ms + EPS) * g2_ref[...]).astype(BF16)

    @pl.when(g % 2 == 0)
    def _even():
        body(ya_scr, yb_scr)

    @pl.when(g % 2 == 1)
    def _odd():
        body(yb_scr, ya_scr)

    @pl.when((j == nj - 1) & (g < n_tiles))
    def _final():
        cout_ref[0] = caug_scr[:, :, 0:DV]
        for h in range(N_HEADS):
            nout_ref[0, h:h + 1, :] = caug_scr[h, :, DV:AUG].T[0:1, :]
        mout_ref[0] = m_scr[...]


def _mix(act, scan, x2d, g_mh, w_sp, b_sp, w_out_b, g2, state, batch, seq, ts):
    lc = min(seq, SGU_CHUNK)
    assert seq % ts == 0 and ts % lc == 0 and (ts % BLK == 0 or ts < BLK)
    nj = seq // ts
    n_tiles = batch * nj
    has_state = state is not None
    cur = lambda g: jnp.minimum(g, n_tiles - 1)
    prev = lambda g: jnp.maximum(g - 1, 0)
    row_cur = lambda width: pl.BlockSpec((ts, width), lambda g: (cur(g), 0))
    row_prev = lambda width: pl.BlockSpec((ts, width), lambda g: (prev(g), 0))
    per_b = lambda shape: pl.BlockSpec((1,) + shape, lambda g: (cur(g) // nj,) + (0,) * len(shape))
    if scan.ndim == 2:
        scan_spec = pl.BlockSpec((SCAN_ROWS, ts), lambda g: (0, cur(g)))
    else:
        scan_spec = pl.BlockSpec((1, SCAN_ROWS, ts), lambda g: (cur(g), 0, 0))
    in_specs = [row_cur(D_ACT), scan_spec, row_prev(D_MODEL), _resident((1, D_M)),
                _resident((N_GROUPS, SGU_CHUNK, SGU_CHUNK)), _resident((N_GROUPS, SGU_CHUNK)),
                _resident((D_MODEL, D_MODEL)), _resident((1, D_MODEL))]
    args = [act, scan, x2d, g_mh, w_sp, b_sp, w_out_b, g2]
    if has_state:
        in_specs += [per_b((N_HEADS, DQK, DV)), per_b((N_HEADS, DQK)), per_b((N_GATES, LANES))]
        args += list(state)
    return pl.pallas_call(
        functools.partial(_mix_kernel, ts=ts, lc=lc, nj=nj, n_tiles=n_tiles, has_state=has_state),
        grid=(n_tiles + 1,),
        in_specs=in_specs,
        out_specs=[row_prev(D_MODEL), row_prev(D_MODEL), per_b((N_HEADS, DQK, DV)),
                   per_b((N_HEADS, DQK)), per_b((N_GATES, LANES))],
        out_shape=[jax.ShapeDtypeStruct((batch * seq, D_MODEL), F32),
                   jax.ShapeDtypeStruct((batch * seq, D_MODEL), BF16),
                   jax.ShapeDtypeStruct((batch, N_HEADS, DQK, DV), F32),
                   jax.ShapeDtypeStruct((batch, N_HEADS, DQK), F32),
                   jax.ShapeDtypeStruct((batch, N_GATES, LANES), F32)],
        scratch_shapes=[pltpu.VMEM((N_HEADS, DQK, AUG), F32),
                        pltpu.VMEM((N_GATES, LANES), F32),
                        pltpu.VMEM((ts, D_MODEL), BF16),
                        pltpu.VMEM((ts, D_MODEL), BF16),
                        pltpu.VMEM((N_GROUPS, SGU_CHUNK, SGU_CHUNK), BF16),
                        pltpu.VMEM((N_GROUPS, SGU_CHUNK, GS_W), F32)],
        compiler_params=pltpu.CompilerParams(dimension_semantics=("arbitrary",),
                                             vmem_limit_bytes=VMEM_LIMIT),
        name="mix",
    )(*args)


def _ffn_kernel(xn_ref, x1c_ref, w1_ref, w2_ref, gf_ref, y_ref):
    kf = pl.program_id(1)
    rows = x1c_ref.shape[0]

    def delta():
        hid = jnp.dot(xn_ref[...], w1_ref[...], preferred_element_type=F32)
        hid = jnp.square(jnp.maximum(hid, 0.0)).astype(BF16)
        return jnp.dot(hid, w2_ref[...], preferred_element_type=F32)

    last = pl.num_programs(1) - 1
    slab = pl.ds(pl.multiple_of(kf * rows, rows), rows)

    @pl.when(kf == 0)
    def _first():
        y_ref[...] = delta()
        y_ref[slab, :] += x1c_ref[...]

    @pl.when((kf != 0) & (kf != last))
    def _middle():
        y_ref[...] += delta()
        y_ref[slab, :] += x1c_ref[...]

    @pl.when(kf == last)
    def _last():
        y_ref[slab, :] += x1c_ref[...]
        y = y_ref[...] + delta()
        y_ref[...] = y * _rms_scale(y) * gf_ref[...]


def _ffn(xn, x1, w1_b, w2_b, g_final, tm, kf):
    t = x1.shape[0]
    assert t % tm == 0 and D_FF % kf == 0
    nk = D_FF // kf
    assert tm % nk == 0 and nk >= 2
    return pl.pallas_call(
        _ffn_kernel,
        grid=(t // tm, nk),
        in_specs=[pl.BlockSpec((tm, D_MODEL), lambda i, k: (i, 0)),
                  pl.BlockSpec((tm // nk, D_MODEL), lambda i, k: (i * nk + k, 0)),
                  pl.BlockSpec((D_MODEL, kf), lambda i, k: (0, k)),
                  pl.BlockSpec((kf, D_MODEL), lambda i, k: (k, 0)),
                  _resident((1, D_MODEL))],
        out_specs=pl.BlockSpec((tm, D_MODEL), lambda i, k: (i, 0)),
        out_shape=jax.ShapeDtypeStruct((t, D_MODEL), F32),
        compiler_params=pltpu.CompilerParams(dimension_semantics=("arbitrary", "arbitrary"),
                                             vmem_limit_bytes=VMEM_LIMIT),
        name="ffn",
    )(xn, x1, w1_b, w2_b, g_final)


def kernel(x_prompt, x_sample, state_mlstm_C, state_mlstm_n, state_mlstm_m, w_in, b_gate, g_mh,
           g_sgu, w_sp, b_sp, w_out, g_norm1, g_norm2, w_ff1, w_ff2, g_final):
    depth = w_in.shape[0]
    assert depth == 1, "single-layer step"
    l = 0
    d_in = w_in.shape[2]

    def w_cols(lo, hi):
        return lax.slice(w_in, (l, 0, lo), (l + 1, D_MODEL, hi)).reshape(D_MODEL, hi - lo).astype(BF16)

    w_r = (w_cols(0, N_QKVO), w_cols(N_QKVO + N_GATES, d_in),
           jnp.pad(w_cols(N_QKVO, N_QKVO + N_GATES), ((0, 0), (0, GATE_PAD - N_GATES))))
    bg = jnp.broadcast_to(b_gate[l][:, None], (N_GATES, LANES))
    g1, g2, gf = g_norm1[l][None, :], g_norm2[l][None, :], g_final[None, :]
    gmh, gsgu = g_mh[l][None, :], g_sgu[l][None, :]

    bp, sp, _ = x_prompt.shape
    xp2d = x_prompt.reshape(bp * sp, D_MODEL)
    act, scan, w1_b, w2_b, w_out_b = _inproj(xp2d, g1, w_r, gsgu, bg, tm=512, scan_blk=BLK,
                                             emit_zs=False, cast_srcs=(w_ff1[l], w_ff2[l], w_out[l]))
    x1, xn, c_p, n_p, m_p = _mix(act, scan, xp2d, gmh, w_sp[l], b_sp[l], w_out_b, g2, None,
                                 bp, sp, ts=512)
    yp = _ffn(xn, x1, w1_b, w2_b, gf, tm=1024, kf=1024)

    bs, ss, _ = x_sample.shape
    xs2d = x_sample.reshape(bs * ss, D_MODEL)
    act_s, scan_s, zs_s = _inproj(xs2d, g1, w_r, gsgu, bg, tm=bs * ss, scan_blk=ss, emit_zs=True)
    scan_s = scan_s.reshape(SCAN_ROWS, bs, ss).transpose(1, 0, 2)
    m0 = jnp.broadcast_to(
        jnp.pad(state_mlstm_m[l], ((0, 0), (0, N_GATES - N_HEADS)))[:, :, None],
        (bs, N_GATES, LANES))
    x1s, xns, c_s, n_s, m_s = _mix(act_s, scan_s, xs2d, gmh, w_sp[l], b_sp[l], w_out_b, g2,
                                   (state_mlstm_C[l], state_mlstm_n[l], m0), bs, ss, ts=ss)
    ys = _ffn(xns, x1s, w1_b, w2_b, gf, tm=bs * ss, kf=1024)

    return (yp.reshape(bp, sp, D_MODEL), ys.reshape(bs, ss, D_MODEL),
            c_p[None], n_p[None], m_p[:, 0:N_HEADS, 0][None],
            c_s[None], n_s[None], m_s[:, 0:N_HEADS, 0][None],
            zs_s.reshape(bs, ss, D_S)[None])
```

```python
import functools

import jax
import jax.numpy as jnp
from jax import lax
from jax.experimental import pallas as pl
from jax.experimental.pallas import tpu as pltpu

F32 = jnp.float32
BF16 = jnp.bfloat16

D_MODEL = 2048
N_HEADS = 4
DQK = 128
DV = 256
QK_W = N_HEADS * DQK
D_M = N_HEADS * DV
N_GROUPS = 4
GS_W = 256
D_S = N_GROUPS * GS_W
D_FF = 4 * D_MODEL
SGU_CHUNK = 128
N_GATES = 2 * N_HEADS
EPS = 1e-6

LANES = 128
BLK = 128
GATE_PAD = LANES
N_QKVO = 2 * QK_W + 2 * D_M
OFF_Q, OFF_K, OFF_V = 0, QK_W, 2 * QK_W
OFF_O = OFF_V + D_M
OFF_U, OFF_Z = 0, D_S
AUG = DV + LANES
ACT_Q, ACT_K, ACT_V = 0, QK_W, 2 * QK_W
ACT_O = ACT_V + N_HEADS * AUG
ACT_U = ACT_O + D_M
ACT_Z = ACT_U + D_S
D_ACT = ACT_Z + D_S
OP_CHUNK = 256
MIX_ORDER = ("op:0", "free:0/2", "op:1", "free:1/2", "op:2", "state:0/2", "op:3", "state:1/2",
             "op:4", "intra:0/2", "op:5", "intra:1/2", "norm:0/2", "op:6", "sgu:0", "sgu:1",
             "norm:1/2", "op:7", "sgu:2", "sgu:3")
SCAN_B, SCAN_R, SCAN_CM = 0, N_GATES, 2 * N_GATES
SCAN_ROWS = 3 * N_GATES

VMEM_LIMIT = 56 * 1024 * 1024


def _resident(shape):
    nd = len(shape)
    return pl.BlockSpec(shape, lambda *_: (0,) * nd, pipeline_mode=pl.Buffered(1))


def _rms_scale(x):
    return lax.rsqrt(jnp.mean(x * x, axis=-1, keepdims=True) + EPS)


_LOG2E = 1.4426950408889634
_GELU_C = 0.7978845608028654
_GELU_A = 0.044715


def _gelu(x):
    k1 = -2.0 * _GELU_C * _LOG2E
    k2 = k1 * _GELU_A
    return x / (1.0 + jnp.exp2(x * (k1 + k2 * (x * x))))


def _sigmoid(x):
    return 1.0 / (1.0 + jnp.exp2(x * (-_LOG2E)))


def _block_scan(x, op, fill, blk):
    lane = lax.broadcasted_iota(jnp.int32, x.shape, 1) & (blk - 1)
    shift = 1
    while shift < blk:
        shifted = pltpu.roll(x, shift, axis=1)
        x = op(x, jnp.where(lane >= shift, shifted, fill))
        shift *= 2
    return x


def _inproj_kernel(*refs, scan_blk, emit_zs, cast_weights):
    x_ref, g1_ref, wa_ref, wuz_ref, wg_ref, gsgu_ref, bg_ref = refs[0:7]
    refs = refs[7:]
    if cast_weights:
        wsrc = refs[0:3]
        refs = refs[3:]
    act_ref, scan_ref = refs[0:2]
    refs = refs[2:]
    if emit_zs:
        zs_ref = refs[0]
        refs = refs[1:]
    if cast_weights:
        for src, dst in zip(wsrc, refs):
            dst[...] = src[...].astype(BF16)

    x = x_ref[...]
    h = (x * _rms_scale(x) * g1_ref[...]).astype(BF16)

    def proj(w_ref, off, width):
        return jnp.dot(h, w_ref[:, off:off + width], preferred_element_type=F32)

    g8 = proj(wg_ref, 0, GATE_PAD).T[0:N_GATES, :] + bg_ref[:, 0:1]
    b8 = pltpu.roll(_block_scan(jax.nn.log_sigmoid(g8), jnp.add, 0.0, scan_blk), N_HEADS, axis=0)
    r8 = g8 - b8
    scan_ref[SCAN_B:SCAN_B + N_GATES, :] = b8
    scan_ref[SCAN_R:SCAN_R + N_GATES, :] = r8
    scan_ref[SCAN_CM:SCAN_CM + N_GATES, :] = _block_scan(r8, jnp.maximum, -jnp.inf, scan_blk)

    def group_z():
        zg = _gelu(proj(wuz_ref, OFF_Z, D_S))
        zs = zg * _rms_scale(zg) * gsgu_ref[...]
        act_ref[:, ACT_Z:ACT_Z + D_S] = zs.astype(BF16)
        if emit_zs:
            zs_ref[...] = zs

    def group_u():
        act_ref[:, ACT_U:ACT_U + D_S] = _gelu(proj(wuz_ref, OFF_U, D_S)).astype(BF16)

    def group_o():
        act_ref[:, ACT_O:ACT_O + D_M] = _sigmoid(proj(wa_ref, OFF_O, D_M)).astype(BF16)

    def group_k():
        act_ref[:, ACT_K:ACT_K + QK_W] = (proj(wa_ref, OFF_K, QK_W) * (DQK ** -0.5)).astype(BF16)

    def group_q():
        act_ref[:, ACT_Q:ACT_Q + QK_W] = proj(wa_ref, OFF_Q, QK_W).astype(BF16)

    def group_v():
        v = proj(wa_ref, OFF_V, D_M).astype(BF16)
        ones = jnp.ones((v.shape[0], LANES), BF16)
        for hd in range(N_HEADS):
            act_ref[:, ACT_V + hd * AUG:ACT_V + hd * AUG + DV] = v[:, hd * DV:(hd + 1) * DV]
            act_ref[:, ACT_V + hd * AUG + DV:ACT_V + (hd + 1) * AUG] = ones

    for group in (group_z, group_v, group_u, group_k, group_o, group_q):
        group()


def _inproj(x2d, g1, w_parts, g_sgu, b_gate, tm, scan_blk, emit_zs, cast_srcs=None):
    t = x2d.shape[0]
    assert t % tm == 0
    n = t // tm
    row = lambda width: pl.BlockSpec((tm, width), lambda i: (i, 0))
    in_specs = [row(D_MODEL), _resident((1, D_MODEL)), _resident((D_MODEL, N_QKVO)),
                _resident((D_MODEL, 2 * D_S)), _resident((D_MODEL, GATE_PAD)),
                _resident((1, D_S)), _resident((N_GATES, LANES))]
    args = [x2d, g1, *w_parts, g_sgu, b_gate]
    out_specs = [row(D_ACT), pl.BlockSpec((SCAN_ROWS, tm), lambda i: (0, i))]
    out_shape = [jax.ShapeDtypeStruct((t, D_ACT), BF16), jax.ShapeDtypeStruct((SCAN_ROWS, t), F32)]
    if emit_zs:
        out_specs.append(row(D_S))
        out_shape.append(jax.ShapeDtypeStruct((t, D_S), F32))
    if cast_srcs is not None:
        for w in cast_srcs:
            assert w.shape[0] % n == 0
            slab = pl.BlockSpec((w.shape[0] // n, w.shape[1]), lambda i: (i, 0))
            in_specs.append(slab)
            args.append(w)
            out_specs.append(slab)
            out_shape.append(jax.ShapeDtypeStruct(w.shape, BF16))
    return pl.pallas_call(
        functools.partial(_inproj_kernel, scan_blk=scan_blk, emit_zs=emit_zs,
                          cast_weights=cast_srcs is not None),
        grid=(n,),
        in_specs=in_specs,
        out_specs=out_specs,
        out_shape=out_shape,
        compiler_params=pltpu.CompilerParams(dimension_semantics=("arbitrary",),
                                             vmem_limit_bytes=VMEM_LIMIT),
        name="inproj",
    )(*args)


def _col_bcast(row):
    return jnp.broadcast_to(row, (BLK, BLK)).T


def _pad_rows(a, rows):
    if a.shape[0] == rows:
        return a
    return jnp.concatenate([a, jnp.zeros((rows - a.shape[0],) + a.shape[1:], a.dtype)], axis=0)


def _pad_lanes(a, lanes):
    if a.shape[1] == lanes:
        return a
    return jnp.concatenate([a, jnp.zeros((a.shape[0], lanes - a.shape[1]), a.dtype)], axis=1)


def _mix_kernel(*refs, ts, lc, nj, n_tiles, has_state):
    (act_ref, scan_ref, x_ref, gmh_ref, wsp_ref, bsp_ref, wout_ref, g2_ref) = refs[0:8]
    refs = refs[8:]
    if has_state:
        c0_ref, n0_ref, m0_ref = refs[0:3]
        refs = refs[3:]
    (x1_ref, xn_ref, cout_ref, nout_ref, mout_ref,
     caug_scr, m_scr, ya_scr, yb_scr, wtri_scr, bcol_scr) = refs
    g = pl.program_id(0)
    j = jnp.minimum(g, n_tiles - 1) % nj
    tp = max(ts, BLK)
    nblk = tp // BLK
    valid = min(ts, BLK)

    @pl.when(g == 0)
    def _prepare():
        yb_scr[...] = jnp.zeros(yb_scr.shape, BF16)
        row_i = lax.broadcasted_iota(jnp.int32, (SGU_CHUNK, SGU_CHUNK), 0)
        col_i = lax.broadcasted_iota(jnp.int32, (SGU_CHUNK, SGU_CHUNK), 1)
        for gi in range(N_GROUPS):
            wtri_scr[gi] = jnp.where(col_i <= row_i, wsp_ref[gi], 0.0).astype(BF16)
            b_col = _col_bcast(bsp_ref[gi:gi + 1, :])
            bcol_scr[gi] = jnp.concatenate([b_col, b_col], axis=1)

    @pl.when(j == 0)
    def _init():
        if has_state:
            caug_scr[:, :, 0:DV] = c0_ref[0]
            for h in range(N_HEADS):
                caug_scr[h, :, DV:AUG] = _col_bcast(n0_ref[0, h:h + 1, :])
            m_scr[...] = m0_ref[0]
        else:
            caug_scr[...] = jnp.zeros(caug_scr.shape, F32)
            m_scr[...] = jnp.zeros(m_scr.shape, F32)

    def body(y_cur, y_prev):
        sumsq = []

        def out_proj_chunk(n):
            cols = slice(n * OP_CHUNK, (n + 1) * OP_CHUNK)
            part = x_ref[:, cols] + jnp.dot(y_prev[...], wout_ref[:, cols],
                                            preferred_element_type=F32)
            x1_ref[:, cols] = part
            sumsq.append(jnp.sum(part * part, axis=-1, keepdims=True))

        def sgu_group(gi):
            w_s = wtri_scr[gi, 0:lc, 0:lc]
            b_col = bcol_scr[gi, 0:lc, :]
            for c in range(ts // lc):
                rows = slice(c * lc, (c + 1) * lc)
                z = act_ref[rows, ACT_Z + gi * GS_W:ACT_Z + (gi + 1) * GS_W]
                u = act_ref[rows, ACT_U + gi * GS_W:ACT_U + (gi + 1) * GS_W].astype(F32)
                mixed = jnp.dot(w_s, z, preferred_element_type=F32) + b_col
                y_cur[rows, D_M + gi * GS_W:D_M + (gi + 1) * GS_W] = (u * mixed).astype(BF16)

        scan = scan_ref[...] if scan_ref.ndim == 2 else scan_ref[0]
        scan = _pad_lanes(scan, tp)
        b8 = scan[SCAN_B:SCAN_B + N_GATES]
        r8 = scan[SCAN_R:SCAN_R + N_GATES]
        cm8 = scan[SCAN_CM:SCAN_CM + N_GATES]

        row_i = lax.broadcasted_iota(jnp.int32, (BLK, BLK), 0)
        col_i = lax.broadcasted_iota(jnp.int32, (BLK, BLK), 1)
        causal = col_i <= row_i
        lane8 = lax.broadcasted_iota(jnp.int32, (N_GATES, BLK), 1)

        def load(c, off, width, h):
            rows_v = slice(c * BLK, c * BLK + valid)
            return _pad_rows(act_ref[rows_v, off + h * width:off + (h + 1) * width], BLK)

        def load_v_aug(c, h):
            return load(c, ACT_V, AUG, h)

        gate = []
        m_prev = m_scr[:, 0:1]
        for c in range(nblk):
            lo = c * BLK
            r_c = r8[:, lo:lo + BLK]
            b_c = b8[:, lo:lo + BLK]
            g_c = jnp.maximum(cm8[:, lo:lo + BLK], m_prev)
            g_last = g_c[:, valid - 1:valid]
            gate.append(dict(
                r=r_c, g=g_c, mt=g_c + b_c, m0=m_prev,
                ws=jnp.where(lane8 < valid, jnp.exp(r_c - g_last), 0.0),
                decay=jnp.exp(m_prev - g_last)))
            m_prev = g_last + b_c[:, valid - 1:valid]
        m_scr[...] = jnp.broadcast_to(m_prev, m_scr.shape)

        heads_blocks = [(c, h) for c in range(nblk) for h in range(N_HEADS)]

        s_all, upd_all, inter_all, intra_all, gcol_all = {}, {}, {}, {}, {}

        def part_of(items, part):
            i, n = part
            return items[len(items) * i // n:len(items) * (i + 1) // n]

        def stage_free(part):
            for c, h in part_of(heads_blocks, part):
                k = load(c, ACT_K, DQK, h)
                s_all[c, h] = lax.dot_general(load(c, ACT_Q, DQK, h), k, (((1,), (1,)), ((), ())),
                                              preferred_element_type=F32)
                ws_row = jnp.broadcast_to(gate[c]["ws"][h:h + 1, :], (BLK, BLK))
                kts = (k.astype(F32).T * ws_row).astype(BF16)
                upd_all[c, h] = jnp.dot(kts, load_v_aug(c, h), preferred_element_type=F32)

        def stage_state(part):
            for h in part_of(list(range(N_HEADS)), part):
                caug = caug_scr[h]
                for c in range(nblk):
                    inter_all[c, h] = jnp.dot(load(c, ACT_Q, DQK, h), caug.astype(BF16),
                                              preferred_element_type=F32)
                    caug = gate[c]["decay"][h:h + 1, :] * caug + upd_all[c, h]
                caug_scr[h] = caug

        def stage_intra(part):
            for c, h in part_of(heads_blocks, part):
                g_col = _col_bcast(gate[c]["g"][h:h + 1, :])
                arg = jnp.broadcast_to(gate[c]["r"][h:h + 1, :], (BLK, BLK)) - g_col
                p = jnp.exp(jnp.where(causal, arg, -jnp.inf))
                sw = (s_all[c, h] * p).astype(BF16)
                intra_all[c, h] = jnp.dot(sw, load_v_aug(c, h), preferred_element_type=F32)
                gcol_all[c, h] = g_col

        def stage_norm(part):
            for c, h in part_of(heads_blocks, part):
                rows_v = slice(c * BLK, c * BLK + valid)
                w_col = jnp.exp(gate[c]["m0"][h:h + 1, :] - gcol_all[c, h])
                tot = jnp.concatenate([w_col, w_col, w_col], axis=1) * inter_all[c, h] + intra_all[c, h]
                num = tot[:, 0:DV]
                den = tot[:, DV:AUG]
                dn = jnp.maximum(jnp.abs(den), jnp.exp(-_col_bcast(gate[c]["mt"][h:h + 1, :])))
                hh = num / jnp.concatenate([dn, dn], axis=1)
                hn = hh * _rms_scale(hh) * gmh_ref[:, h * DV:(h + 1) * DV]
                og = act_ref[rows_v, ACT_O + h * DV:ACT_O + (h + 1) * DV].astype(F32)
                y_cur[rows_v, h * DV:(h + 1) * DV] = (hn[0:valid] * og).astype(BF16)

        steps = {"free": stage_free, "state": stage_state, "intra": stage_intra, "norm": stage_norm}
        for name in MIX_ORDER:
            kind, _, arg = name.partition(":")
            if kind == "op":
                out_proj_chunk(int(arg))
            elif kind == "sgu":
                sgu_group(int(arg))
            else:
                i, _, n = arg.partition("/")
                steps[kind]((int(i), int(n)))
        assert len(sumsq) * OP_CHUNK == D_MODEL

        ms = functools.reduce(jnp.add, sumsq) * (1.0 / D_MODEL)
        xn_ref[...] = (x1_ref[...] * lax.rsqrt(ms + EPS) * g2_ref[...]).astype(BF16)

    @pl.when(g % 2 == 0)
    def _even():
        body(ya_scr, yb_scr)

    @pl.when(g % 2 == 1)
    def _odd():
        body(yb_scr, ya_scr)

    @pl.when((j == nj - 1) & (g < n_tiles))
    def _final():
        cout_ref[0] = caug_scr[:, :, 0:DV]
        for h in range(N_HEADS):
            nout_ref[0, h:h + 1, :] = caug_scr[h, :, DV:AUG].T[0:1, :]
        mout_ref[0] = m_scr[...]


def _mix(act, scan, x2d, g_mh, w_sp, b_sp, w_out_b, g2, state, batch, seq, ts):
    lc = min(seq, SGU_CHUNK)
    assert seq % ts == 0 and ts % lc == 0 and (ts % BLK == 0 or ts < BLK)
    nj = seq // ts
    n_tiles = batch * nj
    has_state = state is not None
    cur = lambda g: jnp.minimum(g, n_tiles - 1)
    prev = lambda g: jnp.maximum(g - 1, 0)
    row_cur = lambda width: pl.BlockSpec((ts, width), lambda g: (cur(g), 0))
    row_prev = lambda width: pl.BlockSpec((ts, width), lambda g: (prev(g), 0))
    per_b = lambda shape: pl.BlockSpec((1,) + shape, lambda g: (cur(g) // nj,) + (0,) * len(shape))
    if scan.ndim == 2:
        scan_spec = pl.BlockSpec((SCAN_ROWS, ts), lambda g: (0, cur(g)))
    else:
        scan_spec = pl.BlockSpec((1, SCAN_ROWS, ts), lambda g: (cur(g), 0, 0))
    in_specs = [row_cur(D_ACT), scan_spec, row_prev(D_MODEL), _resident((1, D_M)),
                _resident((N_GROUPS, SGU_CHUNK, SGU_CHUNK)), _resident((N_GROUPS, SGU_CHUNK)),
                _resident((D_MODEL, D_MODEL)), _resident((1, D_MODEL))]
    args = [act, scan, x2d, g_mh, w_sp, b_sp, w_out_b, g2]
    if has_state:
        in_specs += [per_b((N_HEADS, DQK, DV)), per_b((N_HEADS, DQK)), per_b((N_GATES, LANES))]
        args += list(state)
    return pl.pallas_call(
        functools.partial(_mix_kernel, ts=ts, lc=lc, nj=nj, n_tiles=n_tiles, has_state=has_state),
        grid=(n_tiles + 1,),
        in_specs=in_specs,
        out_specs=[row_prev(D_MODEL), row_prev(D_MODEL), per_b((N_HEADS, DQK, DV)),
                   per_b((N_HEADS, DQK)), per_b((N_GATES, LANES))],
        out_shape=[jax.ShapeDtypeStruct((batch * seq, D_MODEL), F32),
                   jax.ShapeDtypeStruct((batch * seq, D_MODEL), BF16),
                   jax.ShapeDtypeStruct((batch, N_HEADS, DQK, DV), F32),
                   jax.ShapeDtypeStruct((batch, N_HEADS, DQK), F32),
                   jax.ShapeDtypeStruct((batch, N_GATES, LANES), F32)],
        scratch_shapes=[pltpu.VMEM((N_HEADS, DQK, AUG), F32),
                        pltpu.VMEM((N_GATES, LANES), F32),
                        pltpu.VMEM((ts, D_MODEL), BF16),
                        pltpu.VMEM((ts, D_MODEL), BF16),
                        pltpu.VMEM((N_GROUPS, SGU_CHUNK, SGU_CHUNK), BF16),
                        pltpu.VMEM((N_GROUPS, SGU_CHUNK, GS_W), F32)],
        compiler_params=pltpu.CompilerParams(dimension_semantics=("arbitrary",),
                                             vmem_limit_bytes=VMEM_LIMIT),
        name="mix",
    )(*args)


def _ffn_kernel(xn_ref, x1c_ref, w1_ref, w2_ref, gf_ref, y_ref):
    kf = pl.program_id(1)
    rows = x1c_ref.shape[0]

    def delta():
        hid = jnp.dot(xn_ref[...], w1_ref[...], preferred_element_type=F32)
        hid = jnp.square(jnp.maximum(hid, 0.0)).astype(BF16)
        return jnp.dot(hid, w2_ref[...], preferred_element_type=F32)

    last = pl.num_programs(1) - 1
    slab = pl.ds(pl.multiple_of(kf * rows, rows), rows)

    @pl.when(kf == 0)
    def _first():
        y_ref[...] = delta()
        y_ref[slab, :] += x1c_ref[...]

    @pl.when((kf != 0) & (kf != last))
    def _middle():
        y_ref[slab, :] += x1c_ref[...]
        y_ref[...] += delta()

    @pl.when(kf == last)
    def _last():
        y_ref[slab, :] += x1c_ref[...]
        y = y_ref[...] + delta()
        y_ref[...] = y * _rms_scale(y) * gf_ref[...]


def _ffn(xn, x1, w1_b, w2_b, g_final, tm, kf):
    t = x1.shape[0]
    assert t % tm == 0 and D_FF % kf == 0
    nk = D_FF // kf
    assert tm % nk == 0 and nk >= 2
    return pl.pallas_call(
        _ffn_kernel,
        grid=(t // tm, nk),
        in_specs=[pl.BlockSpec((tm, D_MODEL), lambda i, k: (i, 0)),
                  pl.BlockSpec((tm // nk, D_MODEL), lambda i, k: (i * nk + k, 0)),
                  pl.BlockSpec((D_MODEL, kf), lambda i, k: (0, k)),
                  pl.BlockSpec((kf, D_MODEL), lambda i, k: (k, 0)),
                  _resident((1, D_MODEL))],
        out_specs=pl.BlockSpec((tm, D_MODEL), lambda i, k: (i, 0)),
        out_shape=jax.ShapeDtypeStruct((t, D_MODEL), F32),
        compiler_params=pltpu.CompilerParams(dimension_semantics=("arbitrary", "arbitrary"),
                                             vmem_limit_bytes=VMEM_LIMIT),
        name="ffn",
    )(xn, x1, w1_b, w2_b, g_final)


def kernel(x_prompt, x_sample, state_mlstm_C, state_mlstm_n, state_mlstm_m, w_in, b_gate, g_mh,
           g_sgu, w_sp, b_sp, w_out, g_norm1, g_norm2, w_ff1, w_ff2, g_final):
    depth = w_in.shape[0]
    assert depth == 1, "single-layer step"
    l = 0
    d_in = w_in.shape[2]

    def w_cols(lo, hi):
        return lax.slice(w_in, (l, 0, lo), (l + 1, D_MODEL, hi)).reshape(D_MODEL, hi - lo).astype(BF16)

    w_r = (w_cols(0, N_QKVO), w_cols(N_QKVO + N_GATES, d_in),
           jnp.pad(w_cols(N_QKVO, N_QKVO + N_GATES), ((0, 0), (0, GATE_PAD - N_GATES))))
    bg = jnp.broadcast_to(b_gate[l][:, None], (N_GATES, LANES))
    g1, g2, gf = g_norm1[l][None, :], g_norm2[l][None, :], g_final[None, :]
    gmh, gsgu = g_mh[l][None, :], g_sgu[l][None, :]

    bp, sp, _ = x_prompt.shape
    xp2d = x_prompt.reshape(bp * sp, D_MODEL)
    act, scan, w1_b, w2_b, w_out_b = _inproj(xp2d, g1, w_r, gsgu, bg, tm=512, scan_blk=BLK,
                                             emit_zs=False, cast_srcs=(w_ff1[l], w_ff2[l], w_out[l]))
    x1, xn, c_p, n_p, m_p = _mix(act, scan, xp2d, gmh, w_sp[l], b_sp[l], w_out_b, g2, None,
                                 bp, sp, ts=512)
    yp = _ffn(xn, x1, w1_b, w2_b, gf, tm=1024, kf=1024)

    bs, ss, _ = x_sample.shape
    xs2d = x_sample.reshape(bs * ss, D_MODEL)
    act_s, scan_s, zs_s = _inproj(xs2d, g1, w_r, gsgu, bg, tm=bs * ss, scan_blk=ss, emit_zs=True)
    scan_s = scan_s.reshape(SCAN_ROWS, bs, ss).transpose(1, 0, 2)
    m0 = jnp.broadcast_to(
        jnp.pad(state_mlstm_m[l], ((0, 0), (0, N_GATES - N_HEADS)))[:, :, None],
        (bs, N_GATES, LANES))
    x1s, xns, c_s, n_s, m_s = _mix(act_s, scan_s, xs2d, gmh, w_sp[l], b_sp[l], w_out_b, g2,
                                   (state_mlstm_C[l], state_mlstm_n[l], m0), bs, ss, ts=ss)
    ys = _ffn(xns, x1s, w1_b, w2_b, gf, tm=bs * ss, kf=1024)

    return (yp.reshape(bp, sp, D_MODEL), ys.reshape(bs, ss, D_MODEL),
            c_p[None], n_p[None], m_p[:, 0:N_HEADS, 0][None],
            c_s[None], n_s[None], m_s[:, 0:N_HEADS, 0][None],
            zs_s.reshape(bs, ss, D_S)[None])
```

```python
import functools

import jax
import jax.numpy as jnp
from jax import lax
from jax.experimental import pallas as pl
from jax.experimental.pallas import tpu as pltpu

F32 = jnp.float32
BF16 = jnp.bfloat16

D_MODEL = 2048
N_HEADS = 4
DQK = 128
DV = 256
QK_W = N_HEADS * DQK
D_M = N_HEADS * DV
N_GROUPS = 4
GS_W = 256
D_S = N_GROUPS * GS_W
D_FF = 4 * D_MODEL
SGU_CHUNK = 128
N_GATES = 2 * N_HEADS
EPS = 1e-6

LANES = 128
BLK = 128
GATE_PAD = LANES
N_QKVO = 2 * QK_W + 2 * D_M
OFF_Q, OFF_K, OFF_V = 0, QK_W, 2 * QK_W
OFF_O = OFF_V + D_M
OFF_U, OFF_Z = 0, D_S
AUG = DV + LANES
ACT_Q, ACT_K, ACT_V = 0, QK_W, 2 * QK_W
ACT_O = ACT_V + N_HEADS * AUG
ACT_U = ACT_O + D_M
ACT_Z = ACT_U + D_S
D_ACT = ACT_Z + D_S
OP_CHUNK = 256
MIX_ORDER = ("op:0", "free:0/2", "op:1", "free:1/2", "op:2", "state:0/2", "op:3", "state:1/2",
             "op:4", "intra:0/2", "op:5", "intra:1/2", "norm:0/2", "op:6", "sgu:0", "sgu:1",
             "norm:1/2", "op:7", "sgu:2", "sgu:3")
SCAN_B, SCAN_R, SCAN_CM = 0, N_GATES, 2 * N_GATES
SCAN_ROWS = 3 * N_GATES

VMEM_LIMIT = 56 * 1024 * 1024
FFN_KF = 1024


def _resident(shape):
    nd = len(shape)
    return pl.BlockSpec(shape, lambda *_: (0,) * nd, pipeline_mode=pl.Buffered(1))


def _rms_scale(x):
    return lax.rsqrt(jnp.mean(x * x, axis=-1, keepdims=True) + EPS)


_LOG2E = 1.4426950408889634
_GELU_C = 0.7978845608028654
_GELU_A = 0.044715


def _gelu(x):
    k1 = -2.0 * _GELU_C * _LOG2E
    k2 = k1 * _GELU_A
    return x / (1.0 + jnp.exp2(x * (k1 + k2 * (x * x))))


def _sigmoid(x):
    return 1.0 / (1.0 + jnp.exp2(x * (-_LOG2E)))


def _block_scan(x, op, fill, blk):
    lane = lax.broadcasted_iota(jnp.int32, x.shape, 1) & (blk - 1)
    shift = 1
    while shift < blk:
        shifted = pltpu.roll(x, shift, axis=1)
        x = op(x, jnp.where(lane >= shift, shifted, fill))
        shift *= 2
    return x


def _inproj_kernel(*refs, scan_blk, emit_zs, cast_weights):
    x_ref, g1_ref, wa_ref, wuz_ref, wg_ref, gsgu_ref, bg_ref = refs[0:7]
    refs = refs[7:]
    if cast_weights:
        wsrc = refs[0:3]
        refs = refs[3:]
    act_ref, scan_ref = refs[0:2]
    refs = refs[2:]
    if emit_zs:
        zs_ref = refs[0]
        refs = refs[1:]
    if cast_weights:
        for src, dst in zip(wsrc, refs):
            if len(dst.shape) == 3:
                width = dst.shape[2]
                for kb in range(dst.shape[0]):
                    dst[kb] = src[:, kb * width:(kb + 1) * width].astype(BF16)
            else:
                dst[...] = src[...].astype(BF16)

    x = x_ref[...]
    h = (x * _rms_scale(x) * g1_ref[...]).astype(BF16)

    def proj(w_ref, off, width):
        return jnp.dot(h, w_ref[:, off:off + width], preferred_element_type=F32)

    g8 = proj(wg_ref, 0, GATE_PAD).T[0:N_GATES, :] + bg_ref[:, 0:1]
    b8 = pltpu.roll(_block_scan(jax.nn.log_sigmoid(g8), jnp.add, 0.0, scan_blk), N_HEADS, axis=0)
    r8 = g8 - b8
    scan_ref[SCAN_B:SCAN_B + N_GATES, :] = b8
    scan_ref[SCAN_R:SCAN_R + N_GATES, :] = r8
    scan_ref[SCAN_CM:SCAN_CM + N_GATES, :] = _block_scan(r8, jnp.maximum, -jnp.inf, scan_blk)

    def group_z():
        zg = _gelu(proj(wuz_ref, OFF_Z, D_S))
        zs = zg * _rms_scale(zg) * gsgu_ref[...]
        act_ref[:, ACT_Z:ACT_Z + D_S] = zs.astype(BF16)
        if emit_zs:
            zs_ref[...] = zs

    def group_u():
        act_ref[:, ACT_U:ACT_U + D_S] = _gelu(proj(wuz_ref, OFF_U, D_S)).astype(BF16)

    def group_o():
        act_ref[:, ACT_O:ACT_O + D_M] = _sigmoid(proj(wa_ref, OFF_O, D_M)).astype(BF16)

    def group_k():
        act_ref[:, ACT_K:ACT_K + QK_W] = (proj(wa_ref, OFF_K, QK_W) * (DQK ** -0.5)).astype(BF16)

    def group_q():
        act_ref[:, ACT_Q:ACT_Q + QK_W] = proj(wa_ref, OFF_Q, QK_W).astype(BF16)

    def group_v():
        v = proj(wa_ref, OFF_V, D_M).astype(BF16)
        ones = jnp.ones((v.shape[0], LANES), BF16)
        for hd in range(N_HEADS):
            act_ref[:, ACT_V + hd * AUG:ACT_V + hd * AUG + DV] = v[:, hd * DV:(hd + 1) * DV]
            act_ref[:, ACT_V + hd * AUG + DV:ACT_V + (hd + 1) * AUG] = ones

    for group in (group_z, group_v, group_u, group_k, group_o, group_q):
        group()


def _inproj(x2d, g1, w_parts, g_sgu, b_gate, tm, scan_blk, emit_zs, cast_srcs=None):
    t = x2d.shape[0]
    assert t % tm == 0
    n = t // tm
    row = lambda width: pl.BlockSpec((tm, width), lambda i: (i, 0))
    in_specs = [row(D_MODEL), _resident((1, D_MODEL)), _resident((D_MODEL, N_QKVO)),
                _resident((D_MODEL, 2 * D_S)), _resident((D_MODEL, GATE_PAD)),
                _resident((1, D_S)), _resident((N_GATES, LANES))]
    args = [x2d, g1, *w_parts, g_sgu, b_gate]
    out_specs = [row(D_ACT), pl.BlockSpec((SCAN_ROWS, tm), lambda i: (0, i))]
    out_shape = [jax.ShapeDtypeStruct((t, D_ACT), BF16), jax.ShapeDtypeStruct((SCAN_ROWS, t), F32)]
    if emit_zs:
        out_specs.append(row(D_S))
        out_shape.append(jax.ShapeDtypeStruct((t, D_S), F32))
    if cast_srcs is not None:
        for w, col_block in cast_srcs:
            assert w.shape[0] % n == 0
            rows = w.shape[0] // n
            slab = pl.BlockSpec((rows, w.shape[1]), lambda i: (i, 0))
            in_specs.append(slab)
            args.append(w)
            if col_block is None:
                out_specs.append(slab)
                out_shape.append(jax.ShapeDtypeStruct(w.shape, BF16))
            else:
                nb = w.shape[1] // col_block
                out_specs.append(pl.BlockSpec((nb, rows, col_block), lambda i: (0, i, 0)))
                out_shape.append(jax.ShapeDtypeStruct((nb, w.shape[0], col_block), BF16))
    return pl.pallas_call(
        functools.partial(_inproj_kernel, scan_blk=scan_blk, emit_zs=emit_zs,
                          cast_weights=cast_srcs is not None),
        grid=(n,),
        in_specs=in_specs,
        out_specs=out_specs,
        out_shape=out_shape,
        compiler_params=pltpu.CompilerParams(dimension_semantics=("arbitrary",),
                                             vmem_limit_bytes=VMEM_LIMIT),
        name="inproj",
    )(*args)


def _col_bcast(row):
    return jnp.broadcast_to(row, (BLK, BLK)).T


def _pad_rows(a, rows):
    if a.shape[0] == rows:
        return a
    return jnp.concatenate([a, jnp.zeros((rows - a.shape[0],) + a.shape[1:], a.dtype)], axis=0)


def _pad_lanes(a, lanes):
    if a.shape[1] == lanes:
        return a
    return jnp.concatenate([a, jnp.zeros((a.shape[0], lanes - a.shape[1]), a.dtype)], axis=1)


def _mix_kernel(*refs, ts, lc, nj, n_tiles, has_state):
    (act_ref, scan_ref, x_ref, gmh_ref, wsp_ref, bsp_ref, wout_ref, g2_ref) = refs[0:8]
    refs = refs[8:]
    if has_state:
        c0_ref, n0_ref, m0_ref = refs[0:3]
        refs = refs[3:]
    (x1_ref, xn_ref, cout_ref, nout_ref, mout_ref,
     caug_scr, m_scr, ya_scr, yb_scr, wtri_scr, bcol_scr) = refs
    g = pl.program_id(0)
    j = jnp.minimum(g, n_tiles - 1) % nj
    tp = max(ts, BLK)
    nblk = tp // BLK
    valid = min(ts, BLK)

    @pl.when(g == 0)
    def _prepare():
        yb_scr[...] = jnp.zeros(yb_scr.shape, BF16)
        row_i = lax.broadcasted_iota(jnp.int32, (SGU_CHUNK, SGU_CHUNK), 0)
        col_i = lax.broadcasted_iota(jnp.int32, (SGU_CHUNK, SGU_CHUNK), 1)
        for gi in range(N_GROUPS):
            wtri_scr[gi] = jnp.where(col_i <= row_i, wsp_ref[gi], 0.0).astype(BF16)
            b_col = _col_bcast(bsp_ref[gi:gi + 1, :])
            bcol_scr[gi] = jnp.concatenate([b_col, b_col], axis=1)

    @pl.when(j == 0)
    def _init():
        if has_state:
            caug_scr[:, :, 0:DV] = c0_ref[0]
            for h in range(N_HEADS):
                caug_scr[h, :, DV:AUG] = _col_bcast(n0_ref[0, h:h + 1, :])
            m_scr[...] = m0_ref[0]
        else:
            caug_scr[...] = jnp.zeros(caug_scr.shape, F32)
            m_scr[...] = jnp.zeros(m_scr.shape, F32)

    def body(y_cur, y_prev):
        sumsq = []

        def out_proj_chunk(n):
            cols = slice(n * OP_CHUNK, (n + 1) * OP_CHUNK)
            part = x_ref[:, cols] + jnp.dot(y_prev[...], wout_ref[:, cols],
                                            preferred_element_type=F32)
            x1_ref[:, cols] = part
            sumsq.append(jnp.sum(part * part, axis=-1, keepdims=True))

        def sgu_group(gi):
            w_s = wtri_scr[gi, 0:lc, 0:lc]
            b_col = bcol_scr[gi, 0:lc, :]
            for c in range(ts // lc):
                rows = slice(c * lc, (c + 1) * lc)
                z = act_ref[rows, ACT_Z + gi * GS_W:ACT_Z + (gi + 1) * GS_W]
                u = act_ref[rows, ACT_U + gi * GS_W:ACT_U + (gi + 1) * GS_W].astype(F32)
                mixed = jnp.dot(w_s, z, preferred_element_type=F32) + b_col
                y_cur[rows, D_M + gi * GS_W:D_M + (gi + 1) * GS_W] = (u * mixed).astype(BF16)

        scan = scan_ref[...] if scan_ref.ndim == 2 else scan_ref[0]
        scan = _pad_lanes(scan, tp)
        b8 = scan[SCAN_B:SCAN_B + N_GATES]
        r8 = scan[SCAN_R:SCAN_R + N_GATES]
        cm8 = scan[SCAN_CM:SCAN_CM + N_GATES]

        row_i = lax.broadcasted_iota(jnp.int32, (BLK, BLK), 0)
        col_i = lax.broadcasted_iota(jnp.int32, (BLK, BLK), 1)
        causal = col_i <= row_i
        lane8 = lax.broadcasted_iota(jnp.int32, (N_GATES, BLK), 1)

        def load(c, off, width, h):
            rows_v = slice(c * BLK, c * BLK + valid)
            return _pad_rows(act_ref[rows_v, off + h * width:off + (h + 1) * width], BLK)

        def load_v_aug(c, h):
            return load(c, ACT_V, AUG, h)

        gate = []
        m_prev = m_scr[:, 0:1]
        for c in range(nblk):
            lo = c * BLK
            r_c = r8[:, lo:lo + BLK]
            b_c = b8[:, lo:lo + BLK]
            g_c = jnp.maximum(cm8[:, lo:lo + BLK], m_prev)
            g_last = g_c[:, valid - 1:valid]
            gate.append(dict(
                r=r_c, g=g_c, mt=g_c + b_c, m0=m_prev,
                ws=jnp.where(lane8 < valid, jnp.exp(r_c - g_last), 0.0),
                decay=jnp.exp(m_prev - g_last)))
            m_prev = g_last + b_c[:, valid - 1:valid]
        m_scr[...] = jnp.broadcast_to(m_prev, m_scr.shape)

        heads_blocks = [(c, h) for c in range(nblk) for h in range(N_HEADS)]

        s_all, upd_all, inter_all, intra_all, gcol_all = {}, {}, {}, {}, {}

        def part_of(items, part):
            i, n = part
            return items[len(items) * i // n:len(items) * (i + 1) // n]

        def stage_free(part):
            for c, h in part_of(heads_blocks, part):
                k = load(c, ACT_K, DQK, h)
                s_all[c, h] = lax.dot_general(load(c, ACT_Q, DQK, h), k, (((1,), (1,)), ((), ())),
                                              preferred_element_type=F32)
                ws_row = jnp.broadcast_to(gate[c]["ws"][h:h + 1, :], (BLK, BLK))
                kts = (k.astype(F32).T * ws_row).astype(BF16)
                upd_all[c, h] = jnp.dot(kts, load_v_aug(c, h), preferred_element_type=F32)

        def stage_state(part):
            for h in part_of(list(range(N_HEADS)), part):
                caug = caug_scr[h]
                for c in range(nblk):
                    inter_all[c, h] = jnp.dot(load(c, ACT_Q, DQK, h), caug.astype(BF16),
                                              preferred_element_type=F32)
                    caug = gate[c]["decay"][h:h + 1, :] * caug + upd_all[c, h]
                caug_scr[h] = caug

        def stage_intra(part):
            for c, h in part_of(heads_blocks, part):
                g_col = _col_bcast(gate[c]["g"][h:h + 1, :])
                arg = jnp.broadcast_to(gate[c]["r"][h:h + 1, :], (BLK, BLK)) - g_col
                p = jnp.exp(jnp.where(causal, arg, -jnp.inf))
                sw = (s_all[c, h] * p).astype(BF16)
                intra_all[c, h] = jnp.dot(sw, load_v_aug(c, h), preferred_element_type=F32)
                gcol_all[c, h] = g_col

        def stage_norm(part):
            for c, h in part_of(heads_blocks, part):
                rows_v = slice(c * BLK, c * BLK + valid)
                w_col = jnp.exp(gate[c]["m0"][h:h + 1, :] - gcol_all[c, h])
                tot = jnp.concatenate([w_col, w_col, w_col], axis=1) * inter_all[c, h] + intra_all[c, h]
                num = tot[:, 0:DV]
                den = tot[:, DV:AUG]
                dn = jnp.maximum(jnp.abs(den), jnp.exp(-_col_bcast(gate[c]["mt"][h:h + 1, :])))
                hh = num / jnp.concatenate([dn, dn], axis=1)
                hn = hh * _rms_scale(hh) * gmh_ref[:, h * DV:(h + 1) * DV]
                og = act_ref[rows_v, ACT_O + h * DV:ACT_O + (h + 1) * DV].astype(F32)
                y_cur[rows_v, h * DV:(h + 1) * DV] = (hn[0:valid] * og).astype(BF16)

        steps = {"free": stage_free, "state": stage_state, "intra": stage_intra, "norm": stage_norm}
        for name in MIX_ORDER:
            kind, _, arg = name.partition(":")
            if kind == "op":
                out_proj_chunk(int(arg))
            elif kind == "sgu":
                sgu_group(int(arg))
            else:
                i, _, n = arg.partition("/")
                steps[kind]((int(i), int(n)))
        assert len(sumsq) * OP_CHUNK == D_MODEL

        ms = functools.reduce(jnp.add, sumsq) * (1.0 / D_MODEL)
        xn_ref[...] = (x1_ref[...] * lax.rsqrt(ms + EPS) * g2_ref[...]).astype(BF16)

    @pl.when(g % 2 == 0)
    def _even():
        body(ya_scr, yb_scr)

    @pl.when(g % 2 == 1)
    def _odd():
        body(yb_scr, ya_scr)

    @pl.when((j == nj - 1) & (g < n_tiles))
    def _final():
        cout_ref[0] = caug_scr[:, :, 0:DV]
        for h in range(N_HEADS):
            nout_ref[0, h:h + 1, :] = caug_scr[h, :, DV:AUG].T[0:1, :]
        mout_ref[0] = m_scr[...]


def _mix(act, scan, x2d, g_mh, w_sp, b_sp, w_out_b, g2, state, batch, seq, ts):
    lc = min(seq, SGU_CHUNK)
    assert seq % ts == 0 and ts % lc == 0 and (ts % BLK == 0 or ts < BLK)
    nj = seq // ts
    n_tiles = batch * nj
    has_state = state is not None
    cur = lambda g: jnp.minimum(g, n_tiles - 1)
    prev = lambda g: jnp.maximum(g - 1, 0)
    row_cur = lambda width: pl.BlockSpec((ts, width), lambda g: (cur(g), 0))
    row_prev = lambda width: pl.BlockSpec((ts, width), lambda g: (prev(g), 0))
    per_b = lambda shape: pl.BlockSpec((1,) + shape, lambda g: (cur(g) // nj,) + (0,) * len(shape))
    if scan.ndim == 2:
        scan_spec = pl.BlockSpec((SCAN_ROWS, ts), lambda g: (0, cur(g)))
    else:
        scan_spec = pl.BlockSpec((1, SCAN_ROWS, ts), lambda g: (cur(g), 0, 0))
    in_specs = [row_cur(D_ACT), scan_spec, row_prev(D_MODEL), _resident((1, D_M)),
                _resident((N_GROUPS, SGU_CHUNK, SGU_CHUNK)), _resident((N_GROUPS, SGU_CHUNK)),
                _resident((D_MODEL, D_MODEL)), _resident((1, D_MODEL))]
    args = [act, scan, x2d, g_mh, w_sp, b_sp, w_out_b, g2]
    if has_state:
        in_specs += [per_b((N_HEADS, DQK, DV)), per_b((N_HEADS, DQK)), per_b((N_GATES, LANES))]
        args += list(state)
    return pl.pallas_call(
        functools.partial(_mix_kernel, ts=ts, lc=lc, nj=nj, n_tiles=n_tiles, has_state=has_state),
        grid=(n_tiles + 1,),
        in_specs=in_specs,
        out_specs=[row_prev(D_MODEL), row_prev(D_MODEL), per_b((N_HEADS, DQK, DV)),
                   per_b((N_HEADS, DQK)), per_b((N_GATES, LANES))],
        out_shape=[jax.ShapeDtypeStruct((batch * seq, D_MODEL), F32),
                   jax.ShapeDtypeStruct((batch * seq, D_MODEL), BF16),
                   jax.ShapeDtypeStruct((batch, N_HEADS, DQK, DV), F32),
                   jax.ShapeDtypeStruct((batch, N_HEADS, DQK), F32),
                   jax.ShapeDtypeStruct((batch, N_GATES, LANES), F32)],
        scratch_shapes=[pltpu.VMEM((N_HEADS, DQK, AUG), F32),
                        pltpu.VMEM((N_GATES, LANES), F32),
                        pltpu.VMEM((ts, D_MODEL), BF16),
                        pltpu.VMEM((ts, D_MODEL), BF16),
                        pltpu.VMEM((N_GROUPS, SGU_CHUNK, SGU_CHUNK), BF16),
                        pltpu.VMEM((N_GROUPS, SGU_CHUNK, GS_W), F32)],
        compiler_params=pltpu.CompilerParams(dimension_semantics=("arbitrary",),
                                             vmem_limit_bytes=VMEM_LIMIT),
        name="mix",
    )(*args)


def _ffn_kernel(xn_ref, x1c_ref, w1_ref, w2_ref, gf_ref, y_ref):
    kf = pl.program_id(1)
    rows = x1c_ref.shape[0]

    def delta():
        hid = jnp.dot(xn_ref[...], w1_ref[0], preferred_element_type=F32)
        hid = jnp.square(jnp.maximum(hid, 0.0)).astype(BF16)
        return jnp.dot(hid, w2_ref[...], preferred_element_type=F32)

    last = pl.num_programs(1) - 1
    slab = pl.ds(pl.multiple_of(kf * rows, rows), rows)

    @pl.when(kf == 0)
    def _first():
        y_ref[...] = delta()
        y_ref[slab, :] += x1c_ref[...]

    @pl.when((kf != 0) & (kf != last))
    def _middle():
        y_ref[...] += delta()
        y_ref[slab, :] += x1c_ref[...]

    @pl.when(kf == last)
    def _last():
        y_ref[slab, :] += x1c_ref[...]
        y = y_ref[...] + delta()
        y_ref[...] = y * _rms_scale(y) * gf_ref[...]


def _ffn(xn, x1, w1_b, w2_b, g_final, tm):
    t = x1.shape[0]
    nk, _, kf = w1_b.shape
    assert t % tm == 0 and nk * kf == D_FF
    assert tm % nk == 0 and nk >= 2
    return pl.pallas_call(
        _ffn_kernel,
        grid=(t // tm, nk),
        in_specs=[pl.BlockSpec((tm, D_MODEL), lambda i, k: (i, 0)),
                  pl.BlockSpec((tm // nk, D_MODEL), lambda i, k: (i * nk + k, 0)),
                  pl.BlockSpec((1, D_MODEL, kf), lambda i, k: (k, 0, 0)),
                  pl.BlockSpec((kf, D_MODEL), lambda i, k: (k, 0)),
                  _resident((1, D_MODEL))],
        out_specs=pl.BlockSpec((tm, D_MODEL), lambda i, k: (i, 0)),
        out_shape=jax.ShapeDtypeStruct((t, D_MODEL), F32),
        compiler_params=pltpu.CompilerParams(dimension_semantics=("arbitrary", "arbitrary"),
                                             vmem_limit_bytes=VMEM_LIMIT),
        name="ffn",
    )(xn, x1, w1_b, w2_b, g_final)


def kernel(x_prompt, x_sample, state_mlstm_C, state_mlstm_n, state_mlstm_m, w_in, b_gate, g_mh,
           g_sgu, w_sp, b_sp, w_out, g_norm1, g_norm2, w_ff1, w_ff2, g_final):
    depth = w_in.shape[0]
    assert depth == 1, "single-layer step"
    l = 0
    d_in = w_in.shape[2]

    def w_cols(lo, hi):
        return lax.slice(w_in, (l, 0, lo), (l + 1, D_MODEL, hi)).reshape(D_MODEL, hi - lo).astype(BF16)

    w_r = (w_cols(0, N_QKVO), w_cols(N_QKVO + N_GATES, d_in),
           jnp.pad(w_cols(N_QKVO, N_QKVO + N_GATES), ((0, 0), (0, GATE_PAD - N_GATES))))
    bg = jnp.broadcast_to(b_gate[l][:, None], (N_GATES, LANES))
    g1, g2, gf = g_norm1[l][None, :], g_norm2[l][None, :], g_final[None, :]
    gmh, gsgu = g_mh[l][None, :], g_sgu[l][None, :]

    bp, sp, _ = x_prompt.shape
    xp2d = x_prompt.reshape(bp * sp, D_MODEL)
    act, scan, w1_b, w2_b, w_out_b = _inproj(xp2d, g1, w_r, gsgu, bg, tm=512, scan_blk=BLK,
                                             emit_zs=False,
                                             cast_srcs=((w_ff1[l], FFN_KF), (w_ff2[l], None),
                                                        (w_out[l], None)))
    x1, xn, c_p, n_p, m_p = _mix(act, scan, xp2d, gmh, w_sp[l], b_sp[l], w_out_b, g2, None,
                                 bp, sp, ts=512)
    yp = _ffn(xn, x1, w1_b, w2_b, gf, tm=1024)

    bs, ss, _ = x_sample.shape
    xs2d = x_sample.reshape(bs * ss, D_MODEL)
    act_s, scan_s, zs_s = _inproj(xs2d, g1, w_r, gsgu, bg, tm=bs * ss, scan_blk=ss, emit_zs=True)
    scan_s = scan_s.reshape(SCAN_ROWS, bs, ss).transpose(1, 0, 2)
    m0 = jnp.broadcast_to(
        jnp.pad(state_mlstm_m[l], ((0, 0), (0, N_GATES - N_HEADS)))[:, :, None],
        (bs, N_GATES, LANES))
    x1s, xns, c_s, n_s, m_s = _mix(act_s, scan_s, xs2d, gmh, w_sp[l], b_sp[l], w_out_b, g2,
                                   (state_mlstm_C[l], state_mlstm_n[l], m0), bs, ss, ts=ss)
    ys = _ffn(xns, x1s, w1_b, w2_b, gf, tm=bs * ss)

    return (yp.reshape(bp, sp, D_MODEL), ys.reshape(bs, ss, D_MODEL),
            c_p[None], n_p[None], m_p[:, 0:N_HEADS, 0][None],
            c_s[None], n_s[None], m_s[:, 0:N_HEADS, 0][None],
            zs_s.reshape(bs, ss, D_S)[None])
```

```python
import functools

import jax
import jax.numpy as jnp
from jax import lax
from jax.experimental import pallas as pl
from jax.experimental.pallas import tpu as pltpu

F32 = jnp.float32
BF16 = jnp.bfloat16

D_MODEL = 2048
N_HEADS = 4
DQK = 128
DV = 256
QK_W = N_HEADS * DQK
D_M = N_HEADS * DV
N_GROUPS = 4
GS_W = 256
D_S = N_GROUPS * GS_W
D_FF = 4 * D_MODEL
SGU_CHUNK = 128
N_GATES = 2 * N_HEADS
EPS = 1e-6

LANES = 128
BLK = 128
BLK_PROMPT = 256
GATE_PAD = LANES
N_QKVO = 2 * QK_W + 2 * D_M
OFF_Q, OFF_K, OFF_V = 0, QK_W, 2 * QK_W
OFF_O = OFF_V + D_M
OFF_U, OFF_Z = 0, D_S
AUG = DV + LANES
ACT_Q, ACT_K, ACT_V = 0, QK_W, 2 * QK_W
ACT_O = ACT_V + N_HEADS * AUG
ACT_U = ACT_O + D_M
ACT_Z = ACT_U + D_S
D_ACT = ACT_Z + D_S
OP_CHUNK = 256
MIX_ORDER = ("op:0", "free:0/2", "op:1", "free:1/2", "op:2", "state:0/2", "op:3", "state:1/2",
             "op:4", "intra:0/2", "op:5", "intra:1/2", "norm:0/2", "op:6", "sgu:0", "sgu:1",
             "norm:1/2", "op:7", "sgu:2", "sgu:3")
SCAN_B, SCAN_R, SCAN_CM = 0, N_GATES, 2 * N_GATES
SCAN_ROWS = 3 * N_GATES

VMEM_LIMIT = 56 * 1024 * 1024
FFN_KF = 1024


def _resident(shape):
    nd = len(shape)
    return pl.BlockSpec(shape, lambda *_: (0,) * nd, pipeline_mode=pl.Buffered(1))


def _rms_scale(x):
    return lax.rsqrt(jnp.mean(x * x, axis=-1, keepdims=True) + EPS)


_LOG2E = 1.4426950408889634
_GELU_C = 0.7978845608028654
_GELU_A = 0.044715


def _gelu(x):
    k1 = -2.0 * _GELU_C * _LOG2E
    k2 = k1 * _GELU_A
    return x / (1.0 + jnp.exp2(x * (k1 + k2 * (x * x))))


def _sigmoid(x):
    return 1.0 / (1.0 + jnp.exp2(x * (-_LOG2E)))


def _block_scan(x, op, fill, blk):
    lane = lax.broadcasted_iota(jnp.int32, x.shape, 1) & (blk - 1)
    shift = 1
    while shift < blk:
        shifted = pltpu.roll(x, shift, axis=1)
        x = op(x, jnp.where(lane >= shift, shifted, fill))
        shift *= 2
    return x


def _inproj_kernel(*refs, scan_blk, emit_zs, cast_weights):
    x_ref, g1_ref, wa_ref, wuz_ref, wg_ref, gsgu_ref, bg_ref = refs[0:7]
    refs = refs[7:]
    if cast_weights:
        wsrc = refs[0:3]
        refs = refs[3:]
    act_ref, scan_ref = refs[0:2]
    refs = refs[2:]
    if emit_zs:
        zs_ref = refs[0]
        refs = refs[1:]
    if cast_weights:
        for src, dst in zip(wsrc, refs):
            if len(dst.shape) == 3:
                width = dst.shape[2]
                for kb in range(dst.shape[0]):
                    dst[kb] = src[:, kb * width:(kb + 1) * width].astype(BF16)
            else:
                dst[...] = src[...].astype(BF16)

    x = x_ref[...]
    h = (x * _rms_scale(x) * g1_ref[...]).astype(BF16)

    def proj(w_ref, off, width):
        return jnp.dot(h, w_ref[:, off:off + width], preferred_element_type=F32)

    g8 = proj(wg_ref, 0, GATE_PAD).T[0:N_GATES, :] + bg_ref[:, 0:1]
    b8 = pltpu.roll(_block_scan(jax.nn.log_sigmoid(g8), jnp.add, 0.0, scan_blk), N_HEADS, axis=0)
    r8 = g8 - b8
    scan_ref[SCAN_B:SCAN_B + N_GATES, :] = b8
    scan_ref[SCAN_R:SCAN_R + N_GATES, :] = r8
    scan_ref[SCAN_CM:SCAN_CM + N_GATES, :] = _block_scan(r8, jnp.maximum, -jnp.inf, scan_blk)

    def group_z():
        zg = _gelu(proj(wuz_ref, OFF_Z, D_S))
        zs = zg * _rms_scale(zg) * gsgu_ref[...]
        act_ref[:, ACT_Z:ACT_Z + D_S] = zs.astype(BF16)
        if emit_zs:
            zs_ref[...] = zs

    def group_u():
        act_ref[:, ACT_U:ACT_U + D_S] = _gelu(proj(wuz_ref, OFF_U, D_S)).astype(BF16)

    def group_o():
        act_ref[:, ACT_O:ACT_O + D_M] = _sigmoid(proj(wa_ref, OFF_O, D_M)).astype(BF16)

    def group_k():
        act_ref[:, ACT_K:ACT_K + QK_W] = (proj(wa_ref, OFF_K, QK_W) * (DQK ** -0.5)).astype(BF16)

    def group_q():
        act_ref[:, ACT_Q:ACT_Q + QK_W] = proj(wa_ref, OFF_Q, QK_W).astype(BF16)

    def group_v():
        v = proj(wa_ref, OFF_V, D_M).astype(BF16)
        ones = jnp.ones((v.shape[0], LANES), BF16)
        for hd in range(N_HEADS):
            act_ref[:, ACT_V + hd * AUG:ACT_V + hd * AUG + DV] = v[:, hd * DV:(hd + 1) * DV]
            act_ref[:, ACT_V + hd * AUG + DV:ACT_V + (hd + 1) * AUG] = ones

    for group in (group_z, group_v, group_u, group_k, group_o, group_q):
        group()


def _inproj(x2d, g1, w_parts, g_sgu, b_gate, tm, scan_blk, emit_zs, cast_srcs=None):
    t = x2d.shape[0]
    assert t % tm == 0
    n = t // tm
    row = lambda width: pl.BlockSpec((tm, width), lambda i: (i, 0))
    in_specs = [row(D_MODEL), _resident((1, D_MODEL)), _resident((D_MODEL, N_QKVO)),
                _resident((D_MODEL, 2 * D_S)), _resident((D_MODEL, GATE_PAD)),
                _resident((1, D_S)), _resident((N_GATES, LANES))]
    args = [x2d, g1, *w_parts, g_sgu, b_gate]
    out_specs = [row(D_ACT), pl.BlockSpec((SCAN_ROWS, tm), lambda i: (0, i))]
    out_shape = [jax.ShapeDtypeStruct((t, D_ACT), BF16), jax.ShapeDtypeStruct((SCAN_ROWS, t), F32)]
    if emit_zs:
        out_specs.append(row(D_S))
        out_shape.append(jax.ShapeDtypeStruct((t, D_S), F32))
    if cast_srcs is not None:
        for w, col_block in cast_srcs:
            assert w.shape[0] % n == 0
            rows = w.shape[0] // n
            slab = pl.BlockSpec((rows, w.shape[1]), lambda i: (i, 0))
            in_specs.append(slab)
            args.append(w)
            if col_block is None:
                out_specs.append(slab)
                out_shape.append(jax.ShapeDtypeStruct(w.shape, BF16))
            else:
                nb = w.shape[1] // col_block
                out_specs.append(pl.BlockSpec((nb, rows, col_block), lambda i: (0, i, 0)))
                out_shape.append(jax.ShapeDtypeStruct((nb, w.shape[0], col_block), BF16))
    return pl.pallas_call(
        functools.partial(_inproj_kernel, scan_blk=scan_blk, emit_zs=emit_zs,
                          cast_weights=cast_srcs is not None),
        grid=(n,),
        in_specs=in_specs,
        out_specs=out_specs,
        out_shape=out_shape,
        compiler_params=pltpu.CompilerParams(dimension_semantics=("arbitrary",),
                                             vmem_limit_bytes=VMEM_LIMIT),
        name="inproj",
    )(*args)


def _col_bcast(row):
    n = row.shape[1]
    return jnp.broadcast_to(row, (LANES, n)).T


def _pad_rows(a, rows):
    if a.shape[0] == rows:
        return a
    return jnp.concatenate([a, jnp.zeros((rows - a.shape[0],) + a.shape[1:], a.dtype)], axis=0)


def _pad_lanes(a, lanes):
    if a.shape[1] == lanes:
        return a
    return jnp.concatenate([a, jnp.zeros((a.shape[0], lanes - a.shape[1]), a.dtype)], axis=1)


def _mix_kernel(*refs, ts, lc, nj, n_tiles, has_state, blk):
    (act_ref, scan_ref, x_ref, gmh_ref, wsp_ref, bsp_ref, wout_ref, g2_ref) = refs[0:8]
    refs = refs[8:]
    if has_state:
        c0_ref, n0_ref, m0_ref = refs[0:3]
        refs = refs[3:]
    (x1_ref, xn_ref, cout_ref, nout_ref, mout_ref,
     caug_scr, m_scr, ya_scr, yb_scr, wtri_scr, bcol_scr) = refs
    g = pl.program_id(0)
    j = jnp.minimum(g, n_tiles - 1) % nj
    tp = max(ts, blk)
    nblk = tp // blk
    valid = min(ts, blk)

    @pl.when(g == 0)
    def _prepare():
        yb_scr[...] = jnp.zeros(yb_scr.shape, BF16)
        row_i = lax.broadcasted_iota(jnp.int32, (SGU_CHUNK, SGU_CHUNK), 0)
        col_i = lax.broadcasted_iota(jnp.int32, (SGU_CHUNK, SGU_CHUNK), 1)
        for gi in range(N_GROUPS):
            wtri_scr[gi] = jnp.where(col_i <= row_i, wsp_ref[gi], 0.0).astype(BF16)
            b_col = _col_bcast(bsp_ref[gi:gi + 1, :])
            bcol_scr[gi] = jnp.concatenate([b_col, b_col], axis=1)

    @pl.when(j == 0)
    def _init():
        if has_state:
            caug_scr[:, :, 0:DV] = c0_ref[0]
            for h in range(N_HEADS):
                caug_scr[h, :, DV:AUG] = _col_bcast(n0_ref[0, h:h + 1, :])
            m_scr[...] = m0_ref[0]
        else:
            caug_scr[...] = jnp.zeros(caug_scr.shape, F32)
            m_scr[...] = jnp.zeros(m_scr.shape, F32)

    def body(y_cur, y_prev):
        sumsq = []

        def out_proj_chunk(n):
            cols = slice(n * OP_CHUNK, (n + 1) * OP_CHUNK)
            part = x_ref[:, cols] + jnp.dot(y_prev[...], wout_ref[:, cols],
                                            preferred_element_type=F32)
            x1_ref[:, cols] = part
            sumsq.append(jnp.sum(part * part, axis=-1, keepdims=True))

        def sgu_group(gi):
            w_s = wtri_scr[gi, 0:lc, 0:lc]
            b_col = bcol_scr[gi, 0:lc, :]
            for c in range(ts // lc):
                rows = slice(c * lc, (c + 1) * lc)
                z = act_ref[rows, ACT_Z + gi * GS_W:ACT_Z + (gi + 1) * GS_W]
                u = act_ref[rows, ACT_U + gi * GS_W:ACT_U + (gi + 1) * GS_W].astype(F32)
                mixed = jnp.dot(w_s, z, preferred_element_type=F32) + b_col
                y_cur[rows, D_M + gi * GS_W:D_M + (gi + 1) * GS_W] = (u * mixed).astype(BF16)

        scan = scan_ref[...] if scan_ref.ndim == 2 else scan_ref[0]
        scan = _pad_lanes(scan, tp)
        b8 = scan[SCAN_B:SCAN_B + N_GATES]
        r8 = scan[SCAN_R:SCAN_R + N_GATES]
        cm8 = scan[SCAN_CM:SCAN_CM + N_GATES]

        row_i = lax.broadcasted_iota(jnp.int32, (blk, blk), 0)
        col_i = lax.broadcasted_iota(jnp.int32, (blk, blk), 1)
        causal = col_i <= row_i
        lane8 = lax.broadcasted_iota(jnp.int32, (N_GATES, blk), 1)

        def load(c, off, width, h):
            rows_v = slice(c * blk, c * blk + valid)
            return _pad_rows(act_ref[rows_v, off + h * width:off + (h + 1) * width], blk)

        def load_v_aug(c, h):
            return load(c, ACT_V, AUG, h)

        gate = []
        m_prev = m_scr[:, 0:1]
        for c in range(nblk):
            lo = c * blk
            r_c = r8[:, lo:lo + blk]
            b_c = b8[:, lo:lo + blk]
            g_c = jnp.maximum(cm8[:, lo:lo + blk], m_prev)
            g_last = g_c[:, valid - 1:valid]
            gate.append(dict(
                r=r_c, g=g_c, mt=g_c + b_c, m0=m_prev,
                ws=jnp.where(lane8 < valid, jnp.exp(r_c - g_last), 0.0),
                decay=jnp.exp(m_prev - g_last)))
            m_prev = g_last + b_c[:, valid - 1:valid]
        m_scr[...] = jnp.broadcast_to(m_prev, m_scr.shape)

        heads_blocks = [(c, h) for c in range(nblk) for h in range(N_HEADS)]

        s_all, upd_all, inter_all, intra_all, gcol_all = {}, {}, {}, {}, {}

        def part_of(items, part):
            i, n = part
            return items[len(items) * i // n:len(items) * (i + 1) // n]

        def stage_free(part):
            for c, h in part_of(heads_blocks, part):
                k = load(c, ACT_K, DQK, h)
                s_all[c, h] = lax.dot_general(load(c, ACT_Q, DQK, h), k, (((1,), (1,)), ((), ())),
                                              preferred_element_type=F32)
                ws_row = jnp.broadcast_to(gate[c]["ws"][h:h + 1, :], (DQK, blk))
                kts = (k.astype(F32).T * ws_row).astype(BF16)
                upd_all[c, h] = jnp.dot(kts, load_v_aug(c, h), preferred_element_type=F32)

        def stage_state(part):
            for h in part_of(list(range(N_HEADS)), part):
                caug = caug_scr[h]
                for c in range(nblk):
                    inter_all[c, h] = jnp.dot(load(c, ACT_Q, DQK, h), caug.astype(BF16),
                                              preferred_element_type=F32)
                    caug = gate[c]["decay"][h:h + 1, :] * caug + upd_all[c, h]
                caug_scr[h] = caug

        def stage_intra(part):
            for c, h in part_of(heads_blocks, part):
                g_col = _col_bcast(gate[c]["g"][h:h + 1, :])
                g_wide = jnp.concatenate([g_col] * (blk // LANES), axis=1)
                arg = jnp.broadcast_to(gate[c]["r"][h:h + 1, :], (blk, blk)) - g_wide
                p = jnp.exp(jnp.where(causal, arg, -jnp.inf))
                sw = (s_all[c, h] * p).astype(BF16)
                intra_all[c, h] = jnp.dot(sw, load_v_aug(c, h), preferred_element_type=F32)
                gcol_all[c, h] = g_col

        def stage_norm(part):
            for c, h in part_of(heads_blocks, part):
                rows_v = slice(c * blk, c * blk + valid)
                w_col = jnp.exp(gate[c]["m0"][h:h + 1, :] - gcol_all[c, h])
                tot = jnp.concatenate([w_col, w_col, w_col], axis=1) * inter_all[c, h] + intra_all[c, h]
                num = tot[:, 0:DV]
                den = tot[:, DV:AUG]
                dn = jnp.maximum(jnp.abs(den), jnp.exp(-_col_bcast(gate[c]["mt"][h:h + 1, :])))
                hh = num / jnp.concatenate([dn, dn], axis=1)
                hn = hh * _rms_scale(hh) * gmh_ref[:, h * DV:(h + 1) * DV]
                og = act_ref[rows_v, ACT_O + h * DV:ACT_O + (h + 1) * DV].astype(F32)
                y_cur[rows_v, h * DV:(h + 1) * DV] = (hn[0:valid] * og).astype(BF16)

        steps = {"free": stage_free, "state": stage_state, "intra": stage_intra, "norm": stage_norm}
        for name in MIX_ORDER:
            kind, _, arg = name.partition(":")
            if kind == "op":
                out_proj_chunk(int(arg))
            elif kind == "sgu":
                sgu_group(int(arg))
            else:
                i, _, n = arg.partition("/")
                steps[kind]((int(i), int(n)))
        assert len(sumsq) * OP_CHUNK == D_MODEL

        ms = functools.reduce(jnp.add, sumsq) * (1.0 / D_MODEL)
        xn_ref[...] = (x1_ref[...] * lax.rsqrt(ms + EPS) * g2_ref[...]).astype(BF16)

    @pl.when(g % 2 == 0)
    def _even():
        body(ya_scr, yb_scr)

    @pl.when(g % 2 == 1)
    def _odd():
        body(yb_scr, ya_scr)

    @pl.when((j == nj - 1) & (g < n_tiles))
    def _final():
        cout_ref[0] = caug_scr[:, :, 0:DV]
        for h in range(N_HEADS):
            nout_ref[0, h:h + 1, :] = caug_scr[h, :, DV:AUG].T[0:1, :]
        mout_ref[0] = m_scr[...]


def _mix(act, scan, x2d, g_mh, w_sp, b_sp, w_out_b, g2, state, batch, seq, ts, blk):
    lc = min(seq, SGU_CHUNK)
    assert seq % ts == 0 and ts % lc == 0 and (ts % blk == 0 or ts < blk) and blk % LANES == 0
    nj = seq // ts
    n_tiles = batch * nj
    has_state = state is not None
    cur = lambda g: jnp.minimum(g, n_tiles - 1)
    prev = lambda g: jnp.maximum(g - 1, 0)
    row_cur = lambda width: pl.BlockSpec((ts, width), lambda g: (cur(g), 0))
    row_prev = lambda width: pl.BlockSpec((ts, width), lambda g: (prev(g), 0))
    per_b = lambda shape: pl.BlockSpec((1,) + shape, lambda g: (cur(g) // nj,) + (0,) * len(shape))
    if scan.ndim == 2:
        scan_spec = pl.BlockSpec((SCAN_ROWS, ts), lambda g: (0, cur(g)))
    else:
        scan_spec = pl.BlockSpec((1, SCAN_ROWS, ts), lambda g: (cur(g), 0, 0))
    in_specs = [row_cur(D_ACT), scan_spec, row_prev(D_MODEL), _resident((1, D_M)),
                _resident((N_GROUPS, SGU_CHUNK, SGU_CHUNK)), _resident((N_GROUPS, SGU_CHUNK)),
                _resident((D_MODEL, D_MODEL)), _resident((1, D_MODEL))]
    args = [act, scan, x2d, g_mh, w_sp, b_sp, w_out_b, g2]
    if has_state:
        in_specs += [per_b((N_HEADS, DQK, DV)), per_b((N_HEADS, DQK)), per_b((N_GATES, LANES))]
        args += list(state)
    return pl.pallas_call(
        functools.partial(_mix_kernel, ts=ts, lc=lc, nj=nj, n_tiles=n_tiles, has_state=has_state,
                          blk=blk),
        grid=(n_tiles + 1,),
        in_specs=in_specs,
        out_specs=[row_prev(D_MODEL), row_prev(D_MODEL), per_b((N_HEADS, DQK, DV)),
                   per_b((N_HEADS, DQK)), per_b((N_GATES, LANES))],
        out_shape=[jax.ShapeDtypeStruct((batch * seq, D_MODEL), F32),
                   jax.ShapeDtypeStruct((batch * seq, D_MODEL), BF16),
                   jax.ShapeDtypeStruct((batch, N_HEADS, DQK, DV), F32),
                   jax.ShapeDtypeStruct((batch, N_HEADS, DQK), F32),
                   jax.ShapeDtypeStruct((batch, N_GATES, LANES), F32)],
        scratch_shapes=[pltpu.VMEM((N_HEADS, DQK, AUG), F32),
                        pltpu.VMEM((N_GATES, LANES), F32),
                        pltpu.VMEM((ts, D_MODEL), BF16),
                        pltpu.VMEM((ts, D_MODEL), BF16),
                        pltpu.VMEM((N_GROUPS, SGU_CHUNK, SGU_CHUNK), BF16),
                        pltpu.VMEM((N_GROUPS, SGU_CHUNK, GS_W), F32)],
        compiler_params=pltpu.CompilerParams(dimension_semantics=("arbitrary",),
                                             vmem_limit_bytes=VMEM_LIMIT),
        name="mix",
    )(*args)


def _ffn_kernel(xn_ref, x1c_ref, w1_ref, w2_ref, gf_ref, y_ref):
    kf = pl.program_id(1)
    rows = x1c_ref.shape[0]

    def delta():
        hid = jnp.dot(xn_ref[...], w1_ref[0], preferred_element_type=F32)
        hid = jnp.square(jnp.maximum(hid, 0.0)).astype(BF16)
        return jnp.dot(hid, w2_ref[...], preferred_element_type=F32)

    last = pl.num_programs(1) - 1
    slab = pl.ds(pl.multiple_of(kf * rows, rows), rows)

    @pl.when(kf == 0)
    def _first():
        y_ref[...] = delta()
        y_ref[slab, :] += x1c_ref[...]

    @pl.when((kf != 0) & (kf != last))
    def _middle():
        y_ref[...] += delta()
        y_ref[slab, :] += x1c_ref[...]

    @pl.when(kf == last)
    def _last():
        y_ref[slab, :] += x1c_ref[...]
        y = y_ref[...] + delta()
        y_ref[...] = y * _rms_scale(y) * gf_ref[...]


def _ffn(xn, x1, w1_b, w2_b, g_final, tm):
    t = x1.shape[0]
    nk, _, kf = w1_b.shape
    assert t % tm == 0 and nk * kf == D_FF
    assert tm % nk == 0 and nk >= 2
    return pl.pallas_call(
        _ffn_kernel,
        grid=(t // tm, nk),
        in_specs=[pl.BlockSpec((tm, D_MODEL), lambda i, k: (i, 0)),
                  pl.BlockSpec((tm // nk, D_MODEL), lambda i, k: (i * nk + k, 0)),
                  pl.BlockSpec((1, D_MODEL, kf), lambda i, k: (k, 0, 0)),
                  pl.BlockSpec((kf, D_MODEL), lambda i, k: (k, 0)),
                  _resident((1, D_MODEL))],
        out_specs=pl.BlockSpec((tm, D_MODEL), lambda i, k: (i, 0)),
        out_shape=jax.ShapeDtypeStruct((t, D_MODEL), F32),
        compiler_params=pltpu.CompilerParams(dimension_semantics=("arbitrary", "arbitrary"),
                                             vmem_limit_bytes=VMEM_LIMIT),
        name="ffn",
    )(xn, x1, w1_b, w2_b, g_final)


def kernel(x_prompt, x_sample, state_mlstm_C, state_mlstm_n, state_mlstm_m, w_in, b_gate, g_mh,
           g_sgu, w_sp, b_sp, w_out, g_norm1, g_norm2, w_ff1, w_ff2, g_final):
    depth = w_in.shape[0]
    assert depth == 1, "single-layer step"
    l = 0
    d_in = w_in.shape[2]

    def w_cols(lo, hi):
        return lax.slice(w_in, (l, 0, lo), (l + 1, D_MODEL, hi)).reshape(D_MODEL, hi - lo).astype(BF16)

    w_r = (w_cols(0, N_QKVO), w_cols(N_QKVO + N_GATES, d_in),
           jnp.pad(w_cols(N_QKVO, N_QKVO + N_GATES), ((0, 0), (0, GATE_PAD - N_GATES))))
    bg = jnp.broadcast_to(b_gate[l][:, None], (N_GATES, LANES))
    g1, g2, gf = g_norm1[l][None, :], g_norm2[l][None, :], g_final[None, :]
    gmh, gsgu = g_mh[l][None, :], g_sgu[l][None, :]

    bp, sp, _ = x_prompt.shape
    xp2d = x_prompt.reshape(bp * sp, D_MODEL)
    act, scan, w1_b, w2_b, w_out_b = _inproj(xp2d, g1, w_r, gsgu, bg, tm=512, scan_blk=BLK_PROMPT,
                                             emit_zs=False,
                                             cast_srcs=((w_ff1[l], FFN_KF), (w_ff2[l], None),
                                                        (w_out[l], None)))
    x1, xn, c_p, n_p, m_p = _mix(act, scan, xp2d, gmh, w_sp[l], b_sp[l], w_out_b, g2, None,
                                 bp, sp, ts=512, blk=BLK_PROMPT)
    yp = _ffn(xn, x1, w1_b, w2_b, gf, tm=1024)

    bs, ss, _ = x_sample.shape
    xs2d = x_sample.reshape(bs * ss, D_MODEL)
    act_s, scan_s, zs_s = _inproj(xs2d, g1, w_r, gsgu, bg, tm=bs * ss, scan_blk=ss, emit_zs=True)
    scan_s = scan_s.reshape(SCAN_ROWS, bs, ss).transpose(1, 0, 2)
    m0 = jnp.broadcast_to(
        jnp.pad(state_mlstm_m[l], ((0, 0), (0, N_GATES - N_HEADS)))[:, :, None],
        (bs, N_GATES, LANES))
    x1s, xns, c_s, n_s, m_s = _mix(act_s, scan_s, xs2d, gmh, w_sp[l], b_sp[l], w_out_b, g2,
                                   (state_mlstm_C[l], state_mlstm_n[l], m0), bs, ss, ts=ss, blk=BLK)
    ys = _ffn(xns, x1s, w1_b, w2_b, gf, tm=bs * ss)

    return (yp.reshape(bp, sp, D_MODEL), ys.reshape(bs, ss, D_MODEL),
            c_p[None], n_p[None], m_p[:, 0:N_HEADS, 0][None],
            c_s[None], n_s[None], m_s[:, 0:N_HEADS, 0][None],
            zs_s.reshape(bs, ss, D_S)[None])
```

```python
import functools

import jax
import jax.numpy as jnp
from jax import lax
from jax.experimental import pallas as pl
from jax.experimental.pallas import tpu as pltpu

F32 = jnp.float32
BF16 = jnp.bfloat16

D_MODEL = 2048
N_HEADS = 4
DQK = 128
DV = 256
QK_W = N_HEADS * DQK
D_M = N_HEADS * DV
N_GROUPS = 4
GS_W = 256
D_S = N_GROUPS * GS_W
D_FF = 4 * D_MODEL
SGU_CHUNK = 128
N_GATES = 2 * N_HEADS
EPS = 1e-6

LANES = 128
BLK = 128
BLK_PROMPT = 256
GATE_PAD = LANES
N_QKVO = 2 * QK_W + 2 * D_M
W_QKVO, W_UZ, W_G = 0, N_QKVO, N_QKVO + 2 * D_S
W_COLS = W_G + GATE_PAD
OFF_Q, OFF_K, OFF_V = 0, QK_W, 2 * QK_W
OFF_O = OFF_V + D_M
OFF_U, OFF_Z = 0, D_S
AUG = DV + LANES
ACT_Q, ACT_K, ACT_V = 0, QK_W, 2 * QK_W
ACT_O = ACT_V + N_HEADS * AUG
ACT_U = ACT_O + D_M
ACT_Z = ACT_U + D_S
D_ACT = ACT_Z + D_S
OP_CHUNK = 256
MIX_ORDER = ("op:0", "free:0/2", "op:1", "free:1/2", "op:2", "state:0/2", "op:3", "state:1/2",
             "op:4", "intra:0/2", "op:5", "intra:1/2", "norm:0/2", "op:6", "sgu:0", "sgu:1",
             "norm:1/2", "op:7", "sgu:2", "sgu:3")
SCAN_B, SCAN_R, SCAN_CM = 0, N_GATES, 2 * N_GATES
SCAN_ROWS = 3 * N_GATES

VMEM_LIMIT = 56 * 1024 * 1024
FFN_KF = 1024


def _resident(shape):
    nd = len(shape)
    return pl.BlockSpec(shape, lambda *_: (0,) * nd, pipeline_mode=pl.Buffered(1))


def _rms_scale(x):
    return lax.rsqrt(jnp.mean(x * x, axis=-1, keepdims=True) + EPS)


_LOG2E = 1.4426950408889634
_GELU_C = 0.7978845608028654
_GELU_A = 0.044715


def _gelu(x):
    k1 = -2.0 * _GELU_C * _LOG2E
    k2 = k1 * _GELU_A
    return x / (1.0 + jnp.exp2(x * (k1 + k2 * (x * x))))


def _sigmoid(x):
    return 1.0 / (1.0 + jnp.exp2(x * (-_LOG2E)))


def _block_scan(x, op, fill, blk):
    lane = lax.broadcasted_iota(jnp.int32, x.shape, 1) & (blk - 1)
    shift = 1
    while shift < blk:
        shifted = pltpu.roll(x, shift, axis=1)
        x = op(x, jnp.where(lane >= shift, shifted, fill))
        shift *= 2
    return x


def _inproj_kernel(*refs, scan_blk, emit_zs, cast_weights):
    x_ref, g1_ref, w_ref, gsgu_ref, bg_ref = refs[0:5]
    refs = refs[5:]
    wa_ref = w_ref.at[:, W_QKVO:W_QKVO + N_QKVO]
    wuz_ref = w_ref.at[:, W_UZ:W_UZ + 2 * D_S]
    wg_ref = w_ref.at[:, W_G:W_G + GATE_PAD]
    if cast_weights:
        wsrc = refs[0:3]
        refs = refs[3:]
    act_ref, scan_ref = refs[0:2]
    refs = refs[2:]
    if emit_zs:
        zs_ref = refs[0]
        refs = refs[1:]
    if cast_weights:
        for src, dst in zip(wsrc, refs):
            if len(dst.shape) == 3:
                width = dst.shape[2]
                for kb in range(dst.shape[0]):
                    dst[kb] = src[:, kb * width:(kb + 1) * width].astype(BF16)
            else:
                dst[...] = src[...].astype(BF16)

    x = x_ref[...]
    h = (x * _rms_scale(x) * g1_ref[...]).astype(BF16)

    def proj(w_ref, off, width):
        return jnp.dot(h, w_ref[:, off:off + width], preferred_element_type=F32)

    g8 = proj(wg_ref, 0, GATE_PAD).T[0:N_GATES, :] + bg_ref[:, 0:1]
    b8 = pltpu.roll(_block_scan(jax.nn.log_sigmoid(g8), jnp.add, 0.0, scan_blk), N_HEADS, axis=0)
    r8 = g8 - b8
    scan_ref[SCAN_B:SCAN_B + N_GATES, :] = b8
    scan_ref[SCAN_R:SCAN_R + N_GATES, :] = r8
    scan_ref[SCAN_CM:SCAN_CM + N_GATES, :] = _block_scan(r8, jnp.maximum, -jnp.inf, scan_blk)

    def group_z():
        zg = _gelu(proj(wuz_ref, OFF_Z, D_S))
        zs = zg * _rms_scale(zg) * gsgu_ref[...]
        act_ref[:, ACT_Z:ACT_Z + D_S] = zs.astype(BF16)
        if emit_zs:
            zs_ref[...] = zs

    def group_u():
        act_ref[:, ACT_U:ACT_U + D_S] = _gelu(proj(wuz_ref, OFF_U, D_S)).astype(BF16)

    def group_o():
        act_ref[:, ACT_O:ACT_O + D_M] = _sigmoid(proj(wa_ref, OFF_O, D_M)).astype(BF16)

    def group_k():
        act_ref[:, ACT_K:ACT_K + QK_W] = (proj(wa_ref, OFF_K, QK_W) * (DQK ** -0.5)).astype(BF16)

    def group_q():
        act_ref[:, ACT_Q:ACT_Q + QK_W] = proj(wa_ref, OFF_Q, QK_W).astype(BF16)

    def group_v():
        v = proj(wa_ref, OFF_V, D_M).astype(BF16)
        ones = jnp.ones((v.shape[0], LANES), BF16)
        for hd in range(N_HEADS):
            act_ref[:, ACT_V + hd * AUG:ACT_V + hd * AUG + DV] = v[:, hd * DV:(hd + 1) * DV]
            act_ref[:, ACT_V + hd * AUG + DV:ACT_V + (hd + 1) * AUG] = ones

    for group in (group_z, group_v, group_u, group_k, group_o, group_q):
        group()


def _weight_prep_kernel(wt_ref, w_ref):
    w_ref[...] = wt_ref[...].T.astype(BF16)


def _weight_prep(w_t):
    d_in = w_t.shape[0]
    assert d_in == N_QKVO + N_GATES + 2 * D_S and w_t.shape[1] == D_MODEL
    n_a, n_uz = N_QKVO // LANES, 2 * D_S // LANES

    sub = 8
    per_tile = LANES // sub

    def src_row(i):
        return sub * jnp.where(
            i < n_a, i * per_tile,
            jnp.where(i < n_a + n_uz, (N_QKVO + N_GATES) // sub + (i - n_a) * per_tile, N_QKVO // sub))

    return pl.pallas_call(
        _weight_prep_kernel,
        grid=(W_COLS // LANES,),
        in_specs=[pl.BlockSpec((pl.Element(LANES), pl.Element(D_MODEL)), lambda i: (src_row(i), 0))],
        out_specs=pl.BlockSpec((D_MODEL, LANES), lambda i: (0, i)),
        out_shape=jax.ShapeDtypeStruct((D_MODEL, W_COLS), BF16),
        compiler_params=pltpu.CompilerParams(dimension_semantics=("arbitrary",)),
        name="wprep",
    )(w_t)


def _inproj(x2d, g1, w_all, g_sgu, b_gate, tm, scan_blk, emit_zs, cast_srcs=None):
    t = x2d.shape[0]
    assert t % tm == 0
    n = t // tm
    row = lambda width: pl.BlockSpec((tm, width), lambda i: (i, 0))
    in_specs = [row(D_MODEL), _resident((1, D_MODEL)), _resident((D_MODEL, W_COLS)),
                _resident((1, D_S)), _resident((N_GATES, LANES))]
    args = [x2d, g1, w_all, g_sgu, b_gate]
    out_specs = [row(D_ACT), pl.BlockSpec((SCAN_ROWS, tm), lambda i: (0, i))]
    out_shape = [jax.ShapeDtypeStruct((t, D_ACT), BF16), jax.ShapeDtypeStruct((SCAN_ROWS, t), F32)]
    if emit_zs:
        out_specs.append(row(D_S))
        out_shape.append(jax.ShapeDtypeStruct((t, D_S), F32))
    if cast_srcs is not None:
        for w, col_block in cast_srcs:
            assert w.shape[0] % n == 0
            rows = w.shape[0] // n
            slab = pl.BlockSpec((rows, w.shape[1]), lambda i: (i, 0))
            in_specs.append(slab)
            args.append(w)
            if col_block is None:
                out_specs.append(slab)
                out_shape.append(jax.ShapeDtypeStruct(w.shape, BF16))
            else:
                nb = w.shape[1] // col_block
                out_specs.append(pl.BlockSpec((nb, rows, col_block), lambda i: (0, i, 0)))
                out_shape.append(jax.ShapeDtypeStruct((nb, w.shape[0], col_block), BF16))
    return pl.pallas_call(
        functools.partial(_inproj_kernel, scan_blk=scan_blk, emit_zs=emit_zs,
                          cast_weights=cast_srcs is not None),
        grid=(n,),
        in_specs=in_specs,
        out_specs=out_specs,
        out_shape=out_shape,
        compiler_params=pltpu.CompilerParams(dimension_semantics=("arbitrary",),
                                             vmem_limit_bytes=VMEM_LIMIT),
        name="inproj",
    )(*args)


def _col_bcast(row):
    n = row.shape[1]
    return jnp.broadcast_to(row, (LANES, n)).T


def _pad_rows(a, rows):
    if a.shape[0] == rows:
        return a
    return jnp.concatenate([a, jnp.zeros((rows - a.shape[0],) + a.shape[1:], a.dtype)], axis=0)


def _pad_lanes(a, lanes):
    if a.shape[1] == lanes:
        return a
    return jnp.concatenate([a, jnp.zeros((a.shape[0], lanes - a.shape[1]), a.dtype)], axis=1)


def _mix_kernel(*refs, ts, lc, nj, n_tiles, has_state, blk):
    (act_ref, scan_ref, x_ref, gmh_ref, wsp_ref, bsp_ref, wout_ref, g2_ref) = refs[0:8]
    refs = refs[8:]
    if has_state:
        c0_ref, n0_ref, m0_ref = refs[0:3]
        refs = refs[3:]
    (x1_ref, xn_ref, cout_ref, nout_ref, mout_ref,
     caug_scr, m_scr, ya_scr, yb_scr, wtri_scr, bcol_scr) = refs
    g = pl.program_id(0)
    j = jnp.minimum(g, n_tiles - 1) % nj
    tp = max(ts, blk)
    nblk = tp // blk
    valid = min(ts, blk)

    @pl.when(g == 0)
    def _prepare():
        yb_scr[...] = jnp.zeros(yb_scr.shape, BF16)
        row_i = lax.broadcasted_iota(jnp.int32, (SGU_CHUNK, SGU_CHUNK), 0)
        col_i = lax.broadcasted_iota(jnp.int32, (SGU_CHUNK, SGU_CHUNK), 1)
        for gi in range(N_GROUPS):
            wtri_scr[gi] = jnp.where(col_i <= row_i, wsp_ref[gi], 0.0).astype(BF16)
            b_col = _col_bcast(bsp_ref[gi:gi + 1, :])
            bcol_scr[gi] = jnp.concatenate([b_col, b_col], axis=1)

    @pl.when(j == 0)
    def _init():
        if has_state:
            caug_scr[:, :, 0:DV] = c0_ref[0]
            for h in range(N_HEADS):
                caug_scr[h, :, DV:AUG] = _col_bcast(n0_ref[0, h:h + 1, :])
            m_scr[...] = m0_ref[0]
        else:
            caug_scr[...] = jnp.zeros(caug_scr.shape, F32)
            m_scr[...] = jnp.zeros(m_scr.shape, F32)

    def body(y_cur, y_prev):
        sumsq = []

        def out_proj_chunk(n):
            cols = slice(n * OP_CHUNK, (n + 1) * OP_CHUNK)
            part = x_ref[:, cols] + jnp.dot(y_prev[...], wout_ref[:, cols],
                                            preferred_element_type=F32)
            x1_ref[:, cols] = part
            sumsq.append(jnp.sum(part * part, axis=-1, keepdims=True))

        def sgu_group(gi):
            w_s = wtri_scr[gi, 0:lc, 0:lc]
            b_col = bcol_scr[gi, 0:lc, :]
            for c in range(ts // lc):
                rows = slice(c * lc, (c + 1) * lc)
                z = act_ref[rows, ACT_Z + gi * GS_W:ACT_Z + (gi + 1) * GS_W]
                u = act_ref[rows, ACT_U + gi * GS_W:ACT_U + (gi + 1) * GS_W].astype(F32)
                mixed = jnp.dot(w_s, z, preferred_element_type=F32) + b_col
                y_cur[rows, D_M + gi * GS_W:D_M + (gi + 1) * GS_W] = (u * mixed).astype(BF16)

        scan = scan_ref[...] if scan_ref.ndim == 2 else scan_ref[0]
        scan = _pad_lanes(scan, tp)
        b8 = scan[SCAN_B:SCAN_B + N_GATES]
        r8 = scan[SCAN_R:SCAN_R + N_GATES]
        cm8 = scan[SCAN_CM:SCAN_CM + N_GATES]

        row_i = lax.broadcasted_iota(jnp.int32, (blk, blk), 0)
        col_i = lax.broadcasted_iota(jnp.int32, (blk, blk), 1)
        causal = col_i <= row_i
        lane8 = lax.broadcasted_iota(jnp.int32, (N_GATES, blk), 1)

        def load(c, off, width, h):
            rows_v = slice(c * blk, c * blk + valid)
            return _pad_rows(act_ref[rows_v, off + h * width:off + (h + 1) * width], blk)

        def load_v_aug(c, h):
            return load(c, ACT_V, AUG, h)

        gate = []
        m_prev = m_scr[:, 0:1]
        for c in range(nblk):
            lo = c * blk
            r_c = r8[:, lo:lo + blk]
            b_c = b8[:, lo:lo + blk]
            g_c = jnp.maximum(cm8[:, lo:lo + blk], m_prev)
            g_last = g_c[:, valid - 1:valid]
            gate.append(dict(
                r=r_c, g=g_c, mt=g_c + b_c, m0=m_prev,
                ws=jnp.where(lane8 < valid, jnp.exp(r_c - g_last), 0.0),
                decay=jnp.exp(m_prev - g_last)))
            m_prev = g_last + b_c[:, valid - 1:valid]
        m_scr[...] = jnp.broadcast_to(m_prev, m_scr.shape)

        heads_blocks = [(c, h) for c in range(nblk) for h in range(N_HEADS)]

        s_all, upd_all, inter_all, intra_all, gcol_all = {}, {}, {}, {}, {}

        def part_of(items, part):
            i, n = part
            return items[len(items) * i // n:len(items) * (i + 1) // n]

        def stage_free(part):
            for c, h in part_of(heads_blocks, part):
                k = load(c, ACT_K, DQK, h)
                s_all[c, h] = lax.dot_general(load(c, ACT_Q, DQK, h), k, (((1,), (1,)), ((), ())),
                                              preferred_element_type=F32)
                ws_row = jnp.broadcast_to(gate[c]["ws"][h:h + 1, :], (DQK, blk))
                kts = (k.astype(F32).T * ws_row).astype(BF16)
                upd_all[c, h] = jnp.dot(kts, load_v_aug(c, h), preferred_element_type=F32)

        def stage_state(part):
            for h in part_of(list(range(N_HEADS)), part):
                caug = caug_scr[h]
                for c in range(nblk):
                    inter_all[c, h] = jnp.dot(load(c, ACT_Q, DQK, h), caug.astype(BF16),
                                              preferred_element_type=F32)
                    caug = gate[c]["decay"][h:h + 1, :] * caug + upd_all[c, h]
                caug_scr[h] = caug

        def stage_intra(part):
            for c, h in part_of(heads_blocks, part):
                g_col = _col_bcast(gate[c]["g"][h:h + 1, :])
                g_wide = jnp.concatenate([g_col] * (blk // LANES), axis=1)
                arg = jnp.broadcast_to(gate[c]["r"][h:h + 1, :], (blk, blk)) - g_wide
                p = jnp.exp(jnp.where(causal, arg, -jnp.inf))
                sw = (s_all[c, h] * p).astype(BF16)
                intra_all[c, h] = jnp.dot(sw, load_v_aug(c, h), preferred_element_type=F32)
                gcol_all[c, h] = g_col

        def stage_norm(part):
            for c, h in part_of(heads_blocks, part):
                rows_v = slice(c * blk, c * blk + valid)
                w_col = jnp.exp(gate[c]["m0"][h:h + 1, :] - gcol_all[c, h])
                tot = jnp.concatenate([w_col, w_col, w_col], axis=1) * inter_all[c, h] + intra_all[c, h]
                num = tot[:, 0:DV]
                den = tot[:, DV:AUG]
                dn = jnp.maximum(jnp.abs(den), jnp.exp(-_col_bcast(gate[c]["mt"][h:h + 1, :])))
                hh = num / jnp.concatenate([dn, dn], axis=1)
                hn = hh * _rms_scale(hh) * gmh_ref[:, h * DV:(h + 1) * DV]
                og = act_ref[rows_v, ACT_O + h * DV:ACT_O + (h + 1) * DV].astype(F32)
                y_cur[rows_v, h * DV:(h + 1) * DV] = (hn[0:valid] * og).astype(BF16)

        steps = {"free": stage_free, "state": stage_state, "intra": stage_intra, "norm": stage_norm}
        for name in MIX_ORDER:
            kind, _, arg = name.partition(":")
            if kind == "op":
                out_proj_chunk(int(arg))
            elif kind == "sgu":
                sgu_group(int(arg))
            else:
                i, _, n = arg.partition("/")
                steps[kind]((int(i), int(n)))
        assert len(sumsq) * OP_CHUNK == D_MODEL

        ms = functools.reduce(jnp.add, sumsq) * (1.0 / D_MODEL)
        xn_ref[...] = (x1_ref[...] * lax.rsqrt(ms + EPS) * g2_ref[...]).astype(BF16)

    @pl.when(g % 2 == 0)
    def _even():
        body(ya_scr, yb_scr)

    @pl.when(g % 2 == 1)
    def _odd():
        body(yb_scr, ya_scr)

    @pl.when((j == nj - 1) & (g < n_tiles))
    def _final():
        cout_ref[0] = caug_scr[:, :, 0:DV]
        for h in range(N_HEADS):
            nout_ref[0, h:h + 1, :] = caug_scr[h, :, DV:AUG].T[0:1, :]
        mout_ref[0] = m_scr[...]


def _mix(act, scan, x2d, g_mh, w_sp, b_sp, w_out_b, g2, state, batch, seq, ts, blk):
    lc = min(seq, SGU_CHUNK)
    assert seq % ts == 0 and ts % lc == 0 and (ts % blk == 0 or ts < blk) and blk % LANES == 0
    nj = seq // ts
    n_tiles = batch * nj
    has_state = state is not None
    cur = lambda g: jnp.minimum(g, n_tiles - 1)
    prev = lambda g: jnp.maximum(g - 1, 0)
    row_cur = lambda width: pl.BlockSpec((ts, width), lambda g: (cur(g), 0))
    row_prev = lambda width: pl.BlockSpec((ts, width), lambda g: (prev(g), 0))
    per_b = lambda shape: pl.BlockSpec((1,) + shape, lambda g: (cur(g) // nj,) + (0,) * len(shape))
    if scan.ndim == 2:
        scan_spec = pl.BlockSpec((SCAN_ROWS, ts), lambda g: (0, cur(g)))
    else:
        scan_spec = pl.BlockSpec((1, SCAN_ROWS, ts), lambda g: (cur(g), 0, 0))
    in_specs = [row_cur(D_ACT), scan_spec, row_prev(D_MODEL), _resident((1, D_M)),
                _resident((N_GROUPS, SGU_CHUNK, SGU_CHUNK)), _resident((N_GROUPS, SGU_CHUNK)),
                _resident((D_MODEL, D_MODEL)), _resident((1, D_MODEL))]
    args = [act, scan, x2d, g_mh, w_sp, b_sp, w_out_b, g2]
    if has_state:
        in_specs += [per_b((N_HEADS, DQK, DV)), per_b((N_HEADS, DQK)), per_b((N_GATES, LANES))]
        args += list(state)
    return pl.pallas_call(
        functools.partial(_mix_kernel, ts=ts, lc=lc, nj=nj, n_tiles=n_tiles, has_state=has_state,
                          blk=blk),
        grid=(n_tiles + 1,),
        in_specs=in_specs,
        out_specs=[row_prev(D_MODEL), row_prev(D_MODEL), per_b((N_HEADS, DQK, DV)),
                   per_b((N_HEADS, DQK)), per_b((N_GATES, LANES))],
        out_shape=[jax.ShapeDtypeStruct((batch * seq, D_MODEL), F32),
                   jax.ShapeDtypeStruct((batch * seq, D_MODEL), BF16),
                   jax.ShapeDtypeStruct((batch, N_HEADS, DQK, DV), F32),
                   jax.ShapeDtypeStruct((batch, N_HEADS, DQK), F32),
                   jax.ShapeDtypeStruct((batch, N_GATES, LANES), F32)],
        scratch_shapes=[pltpu.VMEM((N_HEADS, DQK, AUG), F32),
                        pltpu.VMEM((N_GATES, LANES), F32),
                        pltpu.VMEM((ts, D_MODEL), BF16),
                        pltpu.VMEM((ts, D_MODEL), BF16),
                        pltpu.VMEM((N_GROUPS, SGU_CHUNK, SGU_CHUNK), BF16),
                        pltpu.VMEM((N_GROUPS, SGU_CHUNK, GS_W), F32)],
        compiler_params=pltpu.CompilerParams(dimension_semantics=("arbitrary",),
                                             vmem_limit_bytes=VMEM_LIMIT),
        name="mix",
    )(*args)


def _ffn_kernel(xn_ref, x1c_ref, w1_ref, w2_ref, gf_ref, y_ref):
    kf = pl.program_id(1)
    rows = x1c_ref.shape[0]

    def delta():
        hid = jnp.dot(xn_ref[...], w1_ref[0], preferred_element_type=F32)
        hid = jnp.square(jnp.maximum(hid, 0.0)).astype(BF16)
        return jnp.dot(hid, w2_ref[...], preferred_element_type=F32)

    last = pl.num_programs(1) - 1
    slab = pl.ds(pl.multiple_of(kf * rows, rows), rows)

    @pl.when(kf == 0)
    def _first():
        y_ref[...] = delta()
        y_ref[slab, :] += x1c_ref[...]

    @pl.when((kf != 0) & (kf != last))
    def _middle():
        y_ref[...] += delta()
        y_ref[slab, :] += x1c_ref[...]

    @pl.when(kf == last)
    def _last():
        y_ref[slab, :] += x1c_ref[...]
        y = y_ref[...] + delta()
        y_ref[...] = y * _rms_scale(y) * gf_ref[...]


def _ffn(xn, x1, w1_b, w2_b, g_final, tm):
    t = x1.shape[0]
    nk, _, kf = w1_b.shape
    assert t % tm == 0 and nk * kf == D_FF
    assert tm % nk == 0 and nk >= 2
    return pl.pallas_call(
        _ffn_kernel,
        grid=(t // tm, nk),
        in_specs=[pl.BlockSpec((tm, D_MODEL), lambda i, k: (i, 0)),
                  pl.BlockSpec((tm // nk, D_MODEL), lambda i, k: (i * nk + k, 0)),
                  pl.BlockSpec((1, D_MODEL, kf), lambda i, k: (k, 0, 0)),
                  pl.BlockSpec((kf, D_MODEL), lambda i, k: (k, 0)),
                  _resident((1, D_MODEL))],
        out_specs=pl.BlockSpec((tm, D_MODEL), lambda i, k: (i, 0)),
        out_shape=jax.ShapeDtypeStruct((t, D_MODEL), F32),
        compiler_params=pltpu.CompilerParams(dimension_semantics=("arbitrary", "arbitrary"),
                                             vmem_limit_bytes=VMEM_LIMIT),
        name="ffn",
    )(xn, x1, w1_b, w2_b, g_final)


def kernel(x_prompt, x_sample, state_mlstm_C, state_mlstm_n, state_mlstm_m, w_in, b_gate, g_mh,
           g_sgu, w_sp, b_sp, w_out, g_norm1, g_norm2, w_ff1, w_ff2, g_final):
    depth = w_in.shape[0]
    assert depth == 1, "single-layer step"
    l = 0
    w_r = _weight_prep(jnp.swapaxes(w_in, 1, 2)[l])
    bg = jnp.broadcast_to(b_gate[l][:, None], (N_GATES, LANES))
    g1, g2, gf = g_norm1[l][None, :], g_norm2[l][None, :], g_final[None, :]
    gmh, gsgu = g_mh[l][None, :], g_sgu[l][None, :]

    bp, sp, _ = x_prompt.shape
    xp2d = x_prompt.reshape(bp * sp, D_MODEL)
    act, scan, w1_b, w2_b, w_out_b = _inproj(xp2d, g1, w_r, gsgu, bg, tm=512, scan_blk=BLK_PROMPT,
                                             emit_zs=False,
                                             cast_srcs=((w_ff1[l], FFN_KF), (w_ff2[l], None),
                                                        (w_out[l], None)))
    x1, xn, c_p, n_p, m_p = _mix(act, scan, xp2d, gmh, w_sp[l], b_sp[l], w_out_b, g2, None,
                                 bp, sp, ts=512, blk=BLK_PROMPT)
    yp = _ffn(xn, x1, w1_b, w2_b, gf, tm=1024)

    bs, ss, _ = x_sample.shape
    xs2d = x_sample.reshape(bs * ss, D_MODEL)
    act_s, scan_s, zs_s = _inproj(xs2d, g1, w_r, gsgu, bg, tm=bs * ss, scan_blk=ss, emit_zs=True)
    scan_s = scan_s.reshape(SCAN_ROWS, bs, ss).transpose(1, 0, 2)
    m0 = jnp.broadcast_to(
        jnp.pad(state_mlstm_m[l], ((0, 0), (0, N_GATES - N_HEADS)))[:, :, None],
        (bs, N_GATES, LANES))
    x1s, xns, c_s, n_s, m_s = _mix(act_s, scan_s, xs2d, gmh, w_sp[l], b_sp[l], w_out_b, g2,
                                   (state_mlstm_C[l], state_mlstm_n[l], m0), bs, ss, ts=ss, blk=BLK)
    ys = _ffn(xns, x1s, w1_b, w2_b, gf, tm=bs * ss)

    return (yp.reshape(bp, sp, D_MODEL), ys.reshape(bs, ss, D_MODEL),
            c_p[None], n_p[None], m_p[:, 0:N_HEADS, 0][None],
            c_s[None], n_s[None], m_s[:, 0:N_HEADS, 0][None],
            zs_s.reshape(bs, ss, D_S)[None])
```

```python
import functools

import jax
import jax.numpy as jnp
from jax import lax
from jax.experimental import pallas as pl
from jax.experimental.pallas import tpu as pltpu

F32 = jnp.float32
BF16 = jnp.bfloat16

D_MODEL = 2048
N_HEADS = 4
DQK = 128
DV = 256
QK_W = N_HEADS * DQK
D_M = N_HEADS * DV
N_GROUPS = 4
GS_W = 256
D_S = N_GROUPS * GS_W
D_FF = 4 * D_MODEL
SGU_CHUNK = 128
N_GATES = 2 * N_HEADS
EPS = 1e-6

LANES = 128
BLK = 128
BLK_PROMPT = 256
GATE_PAD = LANES
N_QKVO = 2 * QK_W + 2 * D_M
W_QKVO, W_UZ, W_G = 0, N_QKVO, N_QKVO + 2 * D_S
PREP_COLS = 512
W_COLS = W_G + PREP_COLS
OFF_Q, OFF_K, OFF_V = 0, QK_W, 2 * QK_W
OFF_O = OFF_V + D_M
OFF_U, OFF_Z = 0, D_S
AUG = DV + LANES
ACT_Q, ACT_K, ACT_V = 0, QK_W, 2 * QK_W
ACT_O = ACT_V + N_HEADS * AUG
ACT_U = ACT_O + D_M
ACT_Z = ACT_U + D_S
D_ACT = ACT_Z + D_S
OP_CHUNK = 256
MIX_ORDER = ("op:0", "free:0/2", "op:1", "free:1/2", "op:2", "state:0/2", "op:3", "state:1/2",
             "op:4", "intra:0/2", "op:5", "intra:1/2", "norm:0/2", "op:6", "sgu:0", "sgu:1",
             "norm:1/2", "op:7", "sgu:2", "sgu:3")
SCAN_B, SCAN_R, SCAN_CM = 0, N_GATES, 2 * N_GATES
SCAN_ROWS = 3 * N_GATES

VMEM_LIMIT = 56 * 1024 * 1024
FFN_KF = 1024


def _resident(shape):
    nd = len(shape)
    return pl.BlockSpec(shape, lambda *_: (0,) * nd, pipeline_mode=pl.Buffered(1))


def _rms_scale(x):
    return lax.rsqrt(jnp.mean(x * x, axis=-1, keepdims=True) + EPS)


_LOG2E = 1.4426950408889634
_GELU_C = 0.7978845608028654
_GELU_A = 0.044715


def _gelu(x):
    k1 = -2.0 * _GELU_C * _LOG2E
    k2 = k1 * _GELU_A
    return x / (1.0 + jnp.exp2(x * (k1 + k2 * (x * x))))


def _sigmoid(x):
    return 1.0 / (1.0 + jnp.exp2(x * (-_LOG2E)))


def _block_scan(x, op, fill, blk):
    lane = lax.broadcasted_iota(jnp.int32, x.shape, 1) & (blk - 1)
    shift = 1
    while shift < blk:
        shifted = pltpu.roll(x, shift, axis=1)
        x = op(x, jnp.where(lane >= shift, shifted, fill))
        shift *= 2
    return x


def _inproj_kernel(*refs, scan_blk, emit_zs, cast_weights):
    x_ref, g1_ref, w_ref, gsgu_ref, bg_ref = refs[0:5]
    refs = refs[5:]
    wa_ref = w_ref.at[:, W_QKVO:W_QKVO + N_QKVO]
    wuz_ref = w_ref.at[:, W_UZ:W_UZ + 2 * D_S]
    wg_ref = w_ref.at[:, W_G:W_G + GATE_PAD]
    if cast_weights:
        wsrc = refs[0:3]
        refs = refs[3:]
    act_ref, scan_ref = refs[0:2]
    refs = refs[2:]
    if emit_zs:
        zs_ref = refs[0]
        refs = refs[1:]
    if cast_weights:
        for src, dst in zip(wsrc, refs):
            if len(dst.shape) == 3:
                width = dst.shape[2]
                for kb in range(dst.shape[0]):
                    dst[kb] = src[:, kb * width:(kb + 1) * width].astype(BF16)
            else:
                dst[...] = src[...].astype(BF16)

    x = x_ref[...]
    h = (x * _rms_scale(x) * g1_ref[...]).astype(BF16)

    def proj(w_ref, off, width):
        return jnp.dot(h, w_ref[:, off:off + width], preferred_element_type=F32)

    g8 = proj(wg_ref, 0, GATE_PAD).T[0:N_GATES, :] + bg_ref[:, 0:1]
    b8 = pltpu.roll(_block_scan(jax.nn.log_sigmoid(g8), jnp.add, 0.0, scan_blk), N_HEADS, axis=0)
    r8 = g8 - b8
    scan_ref[SCAN_B:SCAN_B + N_GATES, :] = b8
    scan_ref[SCAN_R:SCAN_R + N_GATES, :] = r8
    scan_ref[SCAN_CM:SCAN_CM + N_GATES, :] = _block_scan(r8, jnp.maximum, -jnp.inf, scan_blk)

    def group_z():
        zg = _gelu(proj(wuz_ref, OFF_Z, D_S))
        zs = zg * _rms_scale(zg) * gsgu_ref[...]
        act_ref[:, ACT_Z:ACT_Z + D_S] = zs.astype(BF16)
        if emit_zs:
            zs_ref[...] = zs

    def group_u():
        act_ref[:, ACT_U:ACT_U + D_S] = _gelu(proj(wuz_ref, OFF_U, D_S)).astype(BF16)

    def group_o():
        act_ref[:, ACT_O:ACT_O + D_M] = _sigmoid(proj(wa_ref, OFF_O, D_M)).astype(BF16)

    def group_k():
        act_ref[:, ACT_K:ACT_K + QK_W] = (proj(wa_ref, OFF_K, QK_W) * (DQK ** -0.5)).astype(BF16)

    def group_q():
        act_ref[:, ACT_Q:ACT_Q + QK_W] = proj(wa_ref, OFF_Q, QK_W).astype(BF16)

    def group_v():
        v = proj(wa_ref, OFF_V, D_M).astype(BF16)
        ones = jnp.ones((v.shape[0], LANES), BF16)
        for hd in range(N_HEADS):
            act_ref[:, ACT_V + hd * AUG:ACT_V + hd * AUG + DV] = v[:, hd * DV:(hd + 1) * DV]
            act_ref[:, ACT_V + hd * AUG + DV:ACT_V + (hd + 1) * AUG] = ones

    for group in (group_z, group_v, group_u, group_k, group_o, group_q):
        group()


def _weight_prep_kernel(wt_ref, w_ref):
    w_ref[...] = wt_ref[...].T.astype(BF16)


def _weight_prep(w_t):
    d_in = w_t.shape[0]
    assert d_in == N_QKVO + N_GATES + 2 * D_S and w_t.shape[1] == D_MODEL
    assert N_QKVO % PREP_COLS == 0 and (2 * D_S) % PREP_COLS == 0 and N_QKVO + PREP_COLS <= d_in
    n_a, n_uz = N_QKVO // PREP_COLS, 2 * D_S // PREP_COLS

    sub = 8
    per_step = PREP_COLS // sub

    def src_row(i):
        return sub * jnp.where(
            i < n_a, i * per_step,
            jnp.where(i < n_a + n_uz, (N_QKVO + N_GATES) // sub + (i - n_a) * per_step, N_QKVO // sub))

    return pl.pallas_call(
        _weight_prep_kernel,
        grid=(W_COLS // PREP_COLS,),
        in_specs=[pl.BlockSpec((pl.Element(PREP_COLS), pl.Element(D_MODEL)),
                               lambda i: (src_row(i), 0))],
        out_specs=pl.BlockSpec((D_MODEL, PREP_COLS), lambda i: (0, i)),
        out_shape=jax.ShapeDtypeStruct((D_MODEL, W_COLS), BF16),
        compiler_params=pltpu.CompilerParams(dimension_semantics=("arbitrary",)),
        name="wprep",
    )(w_t)


def _inproj(x2d, g1, w_all, g_sgu, b_gate, tm, scan_blk, emit_zs, cast_srcs=None):
    t = x2d.shape[0]
    assert t % tm == 0
    n = t // tm
    row = lambda width: pl.BlockSpec((tm, width), lambda i: (i, 0))
    in_specs = [row(D_MODEL), _resident((1, D_MODEL)), _resident((D_MODEL, W_COLS)),
                _resident((1, D_S)), _resident((N_GATES, LANES))]
    args = [x2d, g1, w_all, g_sgu, b_gate]
    out_specs = [row(D_ACT), pl.BlockSpec((SCAN_ROWS, tm), lambda i: (0, i))]
    out_shape = [jax.ShapeDtypeStruct((t, D_ACT), BF16), jax.ShapeDtypeStruct((SCAN_ROWS, t), F32)]
    if emit_zs:
        out_specs.append(row(D_S))
        out_shape.append(jax.ShapeDtypeStruct((t, D_S), F32))
    if cast_srcs is not None:
        for w, col_block in cast_srcs:
            assert w.shape[0] % n == 0
            rows = w.shape[0] // n
            slab = pl.BlockSpec((rows, w.shape[1]), lambda i: (i, 0))
            in_specs.append(slab)
            args.append(w)
            if col_block is None:
                out_specs.append(slab)
                out_shape.append(jax.ShapeDtypeStruct(w.shape, BF16))
            else:
                nb = w.shape[1] // col_block
                out_specs.append(pl.BlockSpec((nb, rows, col_block), lambda i: (0, i, 0)))
                out_shape.append(jax.ShapeDtypeStruct((nb, w.shape[0], col_block), BF16))
    return pl.pallas_call(
        functools.partial(_inproj_kernel, scan_blk=scan_blk, emit_zs=emit_zs,
                          cast_weights=cast_srcs is not None),
        grid=(n,),
        in_specs=in_specs,
        out_specs=out_specs,
        out_shape=out_shape,
        compiler_params=pltpu.CompilerParams(dimension_semantics=("arbitrary",),
                                             vmem_limit_bytes=VMEM_LIMIT),
        name="inproj",
    )(*args)


def _col_bcast(row):
    n = row.shape[1]
    return jnp.broadcast_to(row, (LANES, n)).T


def _pad_rows(a, rows):
    if a.shape[0] == rows:
        return a
    return jnp.concatenate([a, jnp.zeros((rows - a.shape[0],) + a.shape[1:], a.dtype)], axis=0)


def _pad_lanes(a, lanes):
    if a.shape[1] == lanes:
        return a
    return jnp.concatenate([a, jnp.zeros((a.shape[0], lanes - a.shape[1]), a.dtype)], axis=1)


def _mix_kernel(*refs, ts, lc, nj, n_tiles, has_state, blk):
    (act_ref, scan_ref, x_ref, gmh_ref, wsp_ref, bsp_ref, wout_ref, g2_ref) = refs[0:8]
    refs = refs[8:]
    if has_state:
        c0_ref, n0_ref, m0_ref = refs[0:3]
        refs = refs[3:]
    (x1_ref, xn_ref, cout_ref, nout_ref, mout_ref,
     caug_scr, m_scr, ya_scr, yb_scr, wtri_scr, bcol_scr) = refs
    g = pl.program_id(0)
    j = jnp.minimum(g, n_tiles - 1) % nj
    tp = max(ts, blk)
    nblk = tp // blk
    valid = min(ts, blk)

    @pl.when(g == 0)
    def _prepare():
        yb_scr[...] = jnp.zeros(yb_scr.shape, BF16)
        row_i = lax.broadcasted_iota(jnp.int32, (SGU_CHUNK, SGU_CHUNK), 0)
        col_i = lax.broadcasted_iota(jnp.int32, (SGU_CHUNK, SGU_CHUNK), 1)
        for gi in range(N_GROUPS):
            wtri_scr[gi] = jnp.where(col_i <= row_i, wsp_ref[gi], 0.0).astype(BF16)
            b_col = _col_bcast(bsp_ref[gi:gi + 1, :])
            bcol_scr[gi] = jnp.concatenate([b_col, b_col], axis=1)

    @pl.when(j == 0)
    def _init():
        if has_state:
            caug_scr[:, :, 0:DV] = c0_ref[0]
            for h in range(N_HEADS):
                caug_scr[h, :, DV:AUG] = _col_bcast(n0_ref[0, h:h + 1, :])
            m_scr[...] = m0_ref[0]
        else:
            caug_scr[...] = jnp.zeros(caug_scr.shape, F32)
            m_scr[...] = jnp.zeros(m_scr.shape, F32)

    def body(y_cur, y_prev):
        sumsq = []

        def out_proj_chunk(n):
            cols = slice(n * OP_CHUNK, (n + 1) * OP_CHUNK)
            part = x_ref[:, cols] + jnp.dot(y_prev[...], wout_ref[:, cols],
                                            preferred_element_type=F32)
            x1_ref[:, cols] = part
            sumsq.append(jnp.sum(part * part, axis=-1, keepdims=True))

        def sgu_group(gi):
            w_s = wtri_scr[gi, 0:lc, 0:lc]
            b_col = bcol_scr[gi, 0:lc, :]
            for c in range(ts // lc):
                rows = slice(c * lc, (c + 1) * lc)
                z = act_ref[rows, ACT_Z + gi * GS_W:ACT_Z + (gi + 1) * GS_W]
                u = act_ref[rows, ACT_U + gi * GS_W:ACT_U + (gi + 1) * GS_W].astype(F32)
                mixed = jnp.dot(w_s, z, preferred_element_type=F32) + b_col
                y_cur[rows, D_M + gi * GS_W:D_M + (gi + 1) * GS_W] = (u * mixed).astype(BF16)

        scan = scan_ref[...] if scan_ref.ndim == 2 else scan_ref[0]
        scan = _pad_lanes(scan, tp)
        b8 = scan[SCAN_B:SCAN_B + N_GATES]
        r8 = scan[SCAN_R:SCAN_R + N_GATES]
        cm8 = scan[SCAN_CM:SCAN_CM + N_GATES]

        row_i = lax.broadcasted_iota(jnp.int32, (blk, blk), 0)
        col_i = lax.broadcasted_iota(jnp.int32, (blk, blk), 1)
        causal = col_i <= row_i
        lane8 = lax.broadcasted_iota(jnp.int32, (N_GATES, blk), 1)

        def load(c, off, width, h):
            rows_v = slice(c * blk, c * blk + valid)
            return _pad_rows(act_ref[rows_v, off + h * width:off + (h + 1) * width], blk)

        def load_v_aug(c, h):
            return load(c, ACT_V, AUG, h)

        gate = []
        m_prev = m_scr[:, 0:1]
        for c in range(nblk):
            lo = c * blk
            r_c = r8[:, lo:lo + blk]
            b_c = b8[:, lo:lo + blk]
            g_c = jnp.maximum(cm8[:, lo:lo + blk], m_prev)
            g_last = g_c[:, valid - 1:valid]
            gate.append(dict(
                r=r_c, g=g_c, mt=g_c + b_c, m0=m_prev,
                ws=jnp.where(lane8 < valid, jnp.exp(r_c - g_last), 0.0),
                decay=jnp.exp(m_prev - g_last)))
            m_prev = g_last + b_c[:, valid - 1:valid]
        m_scr[...] = jnp.broadcast_to(m_prev, m_scr.shape)

        heads_blocks = [(c, h) for c in range(nblk) for h in range(N_HEADS)]

        s_all, upd_all, inter_all, intra_all, gcol_all = {}, {}, {}, {}, {}

        def part_of(items, part):
            i, n = part
            return items[len(items) * i // n:len(items) * (i + 1) // n]

        def stage_free(part):
            for c, h in part_of(heads_blocks, part):
                k = load(c, ACT_K, DQK, h)
                s_all[c, h] = lax.dot_general(load(c, ACT_Q, DQK, h), k, (((1,), (1,)), ((), ())),
                                              preferred_element_type=F32)
                ws_row = jnp.broadcast_to(gate[c]["ws"][h:h + 1, :], (DQK, blk))
                kts = (k.astype(F32).T * ws_row).astype(BF16)
                upd_all[c, h] = jnp.dot(kts, load_v_aug(c, h), preferred_element_type=F32)

        def stage_state(part):
            for h in part_of(list(range(N_HEADS)), part):
                caug = caug_scr[h]
                for c in range(nblk):
                    inter_all[c, h] = jnp.dot(load(c, ACT_Q, DQK, h), caug.astype(BF16),
                                              preferred_element_type=F32)
                    caug = gate[c]["decay"][h:h + 1, :] * caug + upd_all[c, h]
                caug_scr[h] = caug

        def stage_intra(part):
            for c, h in part_of(heads_blocks, part):
                g_col = _col_bcast(gate[c]["g"][h:h + 1, :])
                g_wide = jnp.concatenate([g_col] * (blk // LANES), axis=1)
                arg = jnp.broadcast_to(gate[c]["r"][h:h + 1, :], (blk, blk)) - g_wide
                p = jnp.exp(jnp.where(causal, arg, -jnp.inf))
                sw = (s_all[c, h] * p).astype(BF16)
                intra_all[c, h] = jnp.dot(sw, load_v_aug(c, h), preferred_element_type=F32)
                gcol_all[c, h] = g_col

        def stage_norm(part):
            for c, h in part_of(heads_blocks, part):
                rows_v = slice(c * blk, c * blk + valid)
                w_col = jnp.exp(gate[c]["m0"][h:h + 1, :] - gcol_all[c, h])
                tot = jnp.concatenate([w_col, w_col, w_col], axis=1) * inter_all[c, h] + intra_all[c, h]
                num = tot[:, 0:DV]
                den = tot[:, DV:AUG]
                dn = jnp.maximum(jnp.abs(den), jnp.exp(-_col_bcast(gate[c]["mt"][h:h + 1, :])))
                hh = num / jnp.concatenate([dn, dn], axis=1)
                hn = hh * _rms_scale(hh) * gmh_ref[:, h * DV:(h + 1) * DV]
                og = act_ref[rows_v, ACT_O + h * DV:ACT_O + (h + 1) * DV].astype(F32)
                y_cur[rows_v, h * DV:(h + 1) * DV] = (hn[0:valid] * og).astype(BF16)

        steps = {"free": stage_free, "state": stage_state, "intra": stage_intra, "norm": stage_norm}
        for name in MIX_ORDER:
            kind, _, arg = name.partition(":")
            if kind == "op":
                out_proj_chunk(int(arg))
            elif kind == "sgu":
                sgu_group(int(arg))
            else:
                i, _, n = arg.partition("/")
                steps[kind]((int(i), int(n)))
        assert len(sumsq) * OP_CHUNK == D_MODEL

        ms = functools.reduce(jnp.add, sumsq) * (1.0 / D_MODEL)
        xn_ref[...] = (x1_ref[...] * lax.rsqrt(ms + EPS) * g2_ref[...]).astype(BF16)

    @pl.when(g % 2 == 0)
    def _even():
        body(ya_scr, yb_scr)

    @pl.when(g % 2 == 1)
    def _odd():
        body(yb_scr, ya_scr)

    @pl.when((j == nj - 1) & (g < n_tiles))
    def _final():
        cout_ref[0] = caug_scr[:, :, 0:DV]
        for h in range(N_HEADS):
            nout_ref[0, h:h + 1, :] = caug_scr[h, :, DV:AUG].T[0:1, :]
        mout_ref[0] = m_scr[...]


def _mix(act, scan, x2d, g_mh, w_sp, b_sp, w_out_b, g2, state, batch, seq, ts, blk):
    lc = min(seq, SGU_CHUNK)
    assert seq % ts == 0 and ts % lc == 0 and (ts % blk == 0 or ts < blk) and blk % LANES == 0
    nj = seq // ts
    n_tiles = batch * nj
    has_state = state is not None
    cur = lambda g: jnp.minimum(g, n_tiles - 1)
    prev = lambda g: jnp.maximum(g - 1, 0)
    row_cur = lambda width: pl.BlockSpec((ts, width), lambda g: (cur(g), 0))
    row_prev = lambda width: pl.BlockSpec((ts, width), lambda g: (prev(g), 0))
    per_b = lambda shape: pl.BlockSpec((1,) + shape, lambda g: (cur(g) // nj,) + (0,) * len(shape))
    if scan.ndim == 2:
        scan_spec = pl.BlockSpec((SCAN_ROWS, ts), lambda g: (0, cur(g)))
    else:
        scan_spec = pl.BlockSpec((1, SCAN_ROWS, ts), lambda g: (cur(g), 0, 0))
    in_specs = [row_cur(D_ACT), scan_spec, row_prev(D_MODEL), _resident((1, D_M)),
                _resident((N_GROUPS, SGU_CHUNK, SGU_CHUNK)), _resident((N_GROUPS, SGU_CHUNK)),
                _resident((D_MODEL, D_MODEL)), _resident((1, D_MODEL))]
    args = [act, scan, x2d, g_mh, w_sp, b_sp, w_out_b, g2]
    if has_state:
        in_specs += [per_b((N_HEADS, DQK, DV)), per_b((N_HEADS, DQK)), per_b((N_GATES, LANES))]
        args += list(state)
    return pl.pallas_call(
        functools.partial(_mix_kernel, ts=ts, lc=lc, nj=nj, n_tiles=n_tiles, has_state=has_state,
                          blk=blk),
        grid=(n_tiles + 1,),
        in_specs=in_specs,
        out_specs=[row_prev(D_MODEL), row_prev(D_MODEL), per_b((N_HEADS, DQK, DV)),
                   per_b((N_HEADS, DQK)), per_b((N_GATES, LANES))],
        out_shape=[jax.ShapeDtypeStruct((batch * seq, D_MODEL), F32),
                   jax.ShapeDtypeStruct((batch * seq, D_MODEL), BF16),
                   jax.ShapeDtypeStruct((batch, N_HEADS, DQK, DV), F32),
                   jax.ShapeDtypeStruct((batch, N_HEADS, DQK), F32),
                   jax.ShapeDtypeStruct((batch, N_GATES, LANES), F32)],
        scratch_shapes=[pltpu.VMEM((N_HEADS, DQK, AUG), F32),
                        pltpu.VMEM((N_GATES, LANES), F32),
                        pltpu.VMEM((ts, D_MODEL), BF16),
                        pltpu.VMEM((ts, D_MODEL), BF16),
                        pltpu.VMEM((N_GROUPS, SGU_CHUNK, SGU_CHUNK), BF16),
                        pltpu.VMEM((N_GROUPS, SGU_CHUNK, GS_W), F32)],
        compiler_params=pltpu.CompilerParams(dimension_semantics=("arbitrary",),
                                             vmem_limit_bytes=VMEM_LIMIT),
        name="mix",
    )(*args)


def _ffn_kernel(xn_ref, x1c_ref, w1_ref, w2_ref, gf_ref, y_ref):
    kf = pl.program_id(1)
    rows = x1c_ref.shape[0]

    def delta():
        hid = jnp.dot(xn_ref[...], w1_ref[0], preferred_element_type=F32)
        hid = jnp.square(jnp.maximum(hid, 0.0)).astype(BF16)
        return jnp.dot(hid, w2_ref[...], preferred_element_type=F32)

    last = pl.num_programs(1) - 1
    slab = pl.ds(pl.multiple_of(kf * rows, rows), rows)

    @pl.when(kf == 0)
    def _first():
        y_ref[...] = delta()
        y_ref[slab, :] += x1c_ref[...]

    @pl.when((kf != 0) & (kf != last))
    def _middle():
        y_ref[...] += delta()
        y_ref[slab, :] += x1c_ref[...]

    @pl.when(kf == last)
    def _last():
        y_ref[slab, :] += x1c_ref[...]
        y = y_ref[...] + delta()
        y_ref[...] = y * _rms_scale(y) * gf_ref[...]


def _ffn(xn, x1, w1_b, w2_b, g_final, tm):
    t = x1.shape[0]
    nk, _, kf = w1_b.shape
    assert t % tm == 0 and nk * kf == D_FF
    assert tm % nk == 0 and nk >= 2
    return pl.pallas_call(
        _ffn_kernel,
        grid=(t // tm, nk),
        in_specs=[pl.BlockSpec((tm, D_MODEL), lambda i, k: (i, 0)),
                  pl.BlockSpec((tm // nk, D_MODEL), lambda i, k: (i * nk + k, 0)),
                  pl.BlockSpec((1, D_MODEL, kf), lambda i, k: (k, 0, 0)),
                  pl.BlockSpec((kf, D_MODEL), lambda i, k: (k, 0)),
                  _resident((1, D_MODEL))],
        out_specs=pl.BlockSpec((tm, D_MODEL), lambda i, k: (i, 0)),
        out_shape=jax.ShapeDtypeStruct((t, D_MODEL), F32),
        compiler_params=pltpu.CompilerParams(dimension_semantics=("arbitrary", "arbitrary"),
                                             vmem_limit_bytes=VMEM_LIMIT),
        name="ffn",
    )(xn, x1, w1_b, w2_b, g_final)


def kernel(x_prompt, x_sample, state_mlstm_C, state_mlstm_n, state_mlstm_m, w_in, b_gate, g_mh,
           g_sgu, w_sp, b_sp, w_out, g_norm1, g_norm2, w_ff1, w_ff2, g_final):
    depth = w_in.shape[0]
    assert depth == 1, "single-layer step"
    l = 0
    w_r = _weight_prep(jnp.swapaxes(w_in, 1, 2)[l])
    bg = jnp.broadcast_to(b_gate[l][:, None], (N_GATES, LANES))
    g1, g2, gf = g_norm1[l][None, :], g_norm2[l][None, :], g_final[None, :]
    gmh, gsgu = g_mh[l][None, :], g_sgu[l][None, :]

    bp, sp, _ = x_prompt.shape
    xp2d = x_prompt.reshape(bp * sp, D_MODEL)
    act, scan, w1_b, w2_b, w_out_b = _inproj(xp2d, g1, w_r, gsgu, bg, tm=512, scan_blk=BLK_PROMPT,
                                             emit_zs=False,
                                             cast_srcs=((w_ff1[l], FFN_KF), (w_ff2[l], None),
                                                        (w_out[l], None)))
    x1, xn, c_p, n_p, m_p = _mix(act, scan, xp2d, gmh, w_sp[l], b_sp[l], w_out_b, g2, None,
                                 bp, sp, ts=512, blk=BLK_PROMPT)
    yp = _ffn(xn, x1, w1_b, w2_b, gf, tm=1024)

    bs, ss, _ = x_sample.shape
    xs2d = x_sample.reshape(bs * ss, D_MODEL)
    act_s, scan_s, zs_s = _inproj(xs2d, g1, w_r, gsgu, bg, tm=bs * ss, scan_blk=ss, emit_zs=True)
    scan_s = scan_s.reshape(SCAN_ROWS, bs, ss).transpose(1, 0, 2)
    m0 = jnp.broadcast_to(
        jnp.pad(state_mlstm_m[l], ((0, 0), (0, N_GATES - N_HEADS)))[:, :, None],
        (bs, N_GATES, LANES))
    x1s, xns, c_s, n_s, m_s = _mix(act_s, scan_s, xs2d, gmh, w_sp[l], b_sp[l], w_out_b, g2,
                                   (state_mlstm_C[l], state_mlstm_n[l], m0), bs, ss, ts=ss, blk=BLK)
    ys = _ffn(xns, x1s, w1_b, w2_b, gf, tm=bs * ss)

    return (yp.reshape(bp, sp, D_MODEL), ys.reshape(bs, ss, D_MODEL),
            c_p[None], n_p[None], m_p[:, 0:N_HEADS, 0][None],
            c_s[None], n_s[None], m_s[:, 0:N_HEADS, 0][None],
            zs_s.reshape(bs, ss, D_S)[None])
```
